```python
import jax, jax.numpy as jnp
from jax import lax
import numpy as np

D_MODEL = 2048
BATCH = 8
SEQ = 2048
DEPTH = 4
DEC_BATCH = 16
DEC_SEQ = 64
PAST_LEN = 4096

CHUNK = 64
N_LEFT_CHUNKS = 8
ATT_WIN = N_LEFT_CHUNKS * CHUNK
CONV_W = 31
D_CONV = D_MODEL // 2
HEAD_DIM = 128
N_ATT_HEADS = 4
D_ATT = N_ATT_HEADS * HEAD_DIM
N_MEM_HEADS = 4
D_MEM = N_MEM_HEADS * HEAD_DIM
N_MEM = 256
REL_CLIP = 128
D_MIX = D_CONV + D_ATT + D_MEM
IN_WIDTHS = (D_CONV, D_CONV, D_CONV, D_ATT, D_ATT, D_ATT, D_ATT, D_MEM, D_MEM)
D_IN = sum(IN_WIDTHS)
ALPHA = (2 * DEPTH) ** 0.25
BETA = (8 * DEPTH) ** -0.25
LN_EPS = 1e-5

kernel_name = "hymba_streaming_conformer_step"


def layer_norm(x, g, b):
    xf = x.astype(jnp.float32)
    mu = jnp.mean(xf, axis=-1, keepdims=True)
    var = jnp.mean(jnp.square(xf - mu), axis=-1, keepdims=True)
    return ((xf - mu) * lax.rsqrt(var + LN_EPS) * g.astype(jnp.float32) + b.astype(jnp.float32)).astype(x.dtype)


def split_in_proj(h):
    idx = [int(s) for s in np.cumsum(IN_WIDTHS)[:-1]]
    return jnp.split(h, idx, axis=-1)


def conv_branch(a, b, left, conv_w, conv_b, ln_g, ln_b, w_pw):
    u = a * jax.nn.sigmoid(b)
    up = jnp.concatenate([left.astype(u.dtype), u], axis=1)
    y = lax.conv_general_dilated(up, conv_w[:, None, :].astype(u.dtype), window_strides=(1,), padding='VALID',
                                 dimension_numbers=('NWC', 'WIO', 'NWC'), feature_group_count=D_CONV)
    y = layer_norm(y + conv_b, ln_g, ln_b)
    y = jax.nn.silu(y) @ w_pw
    return y, up[:, -(CONV_W - 1):]


def rel_position_bias(table, q_pos, k_pos):
    d = jnp.clip(q_pos[:, None] - k_pos[None, :], -REL_CLIP, REL_CLIP) + REL_CLIP
    return table[:, d].astype(jnp.float32)


def chunk_attention_prompt(q, k, v, table):
    B, T, H, Dh = q.shape
    nc = T // CHUNK
    qc = q.reshape(B, nc, CHUNK, H, Dh)
    pad = jnp.zeros((B, ATT_WIN, H, Dh), k.dtype)
    kp = jnp.concatenate([pad, k], axis=1).reshape(B, nc + N_LEFT_CHUNKS, CHUNK, H, Dh)
    vp = jnp.concatenate([pad, v], axis=1).reshape(B, nc + N_LEFT_CHUNKS, CHUNK, H, Dh)
    kb = jnp.concatenate([kp[:, i:i + nc] for i in range(N_LEFT_CHUNKS + 1)], axis=2)
    vb = jnp.concatenate([vp[:, i:i + nc] for i in range(N_LEFT_CHUNKS + 1)], axis=2)
    band = ATT_WIN + CHUNK
    bias = rel_position_bias(table, jnp.arange(CHUNK) + ATT_WIN, jnp.arange(band))
    valid = (jnp.arange(nc)[:, None] * CHUNK + jnp.arange(band)[None, :]) >= ATT_WIN
    s = jnp.einsum('bcqhd,bckhd->bchqk', qc, kb, preferred_element_type=jnp.float32) * (HEAD_DIM ** -0.5)
    s = jnp.where(valid[None, :, None, None, :], s + bias[None, None], -1e30)
    p = jax.nn.softmax(s, axis=-1)
    o = jnp.einsum('bchqk,bckhd->bcqhd', p.astype(v.dtype), vb)
    return o.reshape(B, T, H * Dh)


def chunk_attention_sample(q, k, v, k_cache, v_cache, table):
    B, T, H, Dh = q.shape
    L = k_cache.shape[1]
    kk = jnp.concatenate([k_cache.astype(k.dtype), k], axis=1)
    vv = jnp.concatenate([v_cache.astype(v.dtype), v], axis=1)
    bias = rel_position_bias(table, jnp.arange(T), jnp.arange(L + T) - L)
    s = jnp.einsum('bqhd,bkhd->bhqk', q, kk, preferred_element_type=jnp.float32) * (HEAD_DIM ** -0.5)
    p = jax.nn.softmax(s + bias[None], axis=-1)
    o = jnp.einsum('bhqk,bkhd->bqhd', p.astype(vv.dtype), vv)
    return o.reshape(B, T, H * Dh)


def memory_attention(q, mk, mv):
    B, T, H, Dh = q.shape
    s = jnp.einsum('bthd,bnhd->bhtn', q, mk.astype(q.dtype), preferred_element_type=jnp.float32) * (HEAD_DIM ** -0.5)
    p = jax.nn.softmax(s, axis=-1)
    o = jnp.einsum('bhtn,bnhd->bthd', p.astype(q.dtype), mv.astype(q.dtype))
    return o.reshape(B, T, H * Dh)


def layer_step(x, mk, mv, conv_left, k_cache, v_cache, w_in, conv_w, conv_b, conv_ln_g, conv_ln_b,
               w_pw, rel_table, w_out, ln_g, ln_b):
    B, T, _ = x.shape
    ca, cb, cg, q, k, v, ag, mq, mg = split_in_proj(x @ w_in)
    conv_out, conv_state = conv_branch(ca, cb, conv_left, conv_w, conv_b, conv_ln_g, conv_ln_b, w_pw)
    q = q.reshape(B, T, N_ATT_HEADS, HEAD_DIM)
    k = k.reshape(B, T, N_ATT_HEADS, HEAD_DIM)
    v = v.reshape(B, T, N_ATT_HEADS, HEAD_DIM)
    if k_cache is None:
        att = chunk_attention_prompt(q, k, v, rel_table)
    else:
        att = chunk_attention_sample(q, k, v, k_cache, v_cache, rel_table)
    mem = memory_attention(mq.reshape(B, T, N_MEM_HEADS, HEAD_DIM), mk, mv)
    mix = jnp.concatenate([conv_out * jax.nn.silu(cg), att * jax.nn.silu(ag), mem * jax.nn.silu(mg)], axis=-1)
    y = layer_norm(ALPHA * x + mix @ w_out, ln_g, ln_b)
    return y, conv_state, k, v


def setup_inputs(seed: int = 0) -> dict:
    key = jax.random.key(seed)
    ks = jax.random.split(key, 20)
    att_cache = min(ATT_WIN, PAST_LEN)

    def nrm(k, shape, s):
        return jax.random.normal(k, shape, jnp.float32) * s

    return {
        "x_prompt": nrm(ks[0], (BATCH, SEQ, D_MODEL), 1.0),
        "x_sample": nrm(ks[1], (DEC_BATCH, DEC_SEQ, D_MODEL), 1.0),
        "mem_prompt": nrm(ks[2], (BATCH, N_MEM, D_MODEL), 1.0),
        "cache_conv": nrm(ks[3], (DEPTH, DEC_BATCH, CONV_W - 1, D_CONV), 0.5),
        "cache_att_k": nrm(ks[4], (DEPTH, DEC_BATCH, att_cache, N_ATT_HEADS, HEAD_DIM), 1.0),
        "cache_att_v": nrm(ks[5], (DEPTH, DEC_BATCH, att_cache, N_ATT_HEADS, HEAD_DIM), 1.0),
        "cache_mem_k": nrm(ks[6], (DEPTH, DEC_BATCH, N_MEM, N_MEM_HEADS, HEAD_DIM), 1.0),
        "cache_mem_v": nrm(ks[7], (DEPTH, DEC_BATCH, N_MEM, N_MEM_HEADS, HEAD_DIM), 1.0),
        "w_in": nrm(ks[8], (DEPTH, D_MODEL, D_IN), D_MODEL ** -0.5),
        "conv_w": nrm(ks[9], (DEPTH, CONV_W, D_CONV), CONV_W ** -0.5),
        "conv_b": nrm(ks[10], (DEPTH, D_CONV), 0.02),
        "conv_ln_g": 1.0 + nrm(ks[11], (DEPTH, D_CONV), 0.05),
        "conv_ln_b": nrm(ks[12], (DEPTH, D_CONV), 0.02),
        "w_pw": nrm(ks[13], (DEPTH, D_CONV, D_CONV), D_CONV ** -0.5),
        "rel_table": nrm(ks[14], (DEPTH, N_ATT_HEADS, 2 * REL_CLIP + 1), 0.1),
        "w_mem_kv": nrm(ks[15], (DEPTH, D_MODEL, 2 * D_MEM), D_MODEL ** -0.5),
        "w_out": nrm(ks[16], (DEPTH, D_MIX, D_MODEL), BETA * D_MIX ** -0.5),
        "ln_g": 1.0 + nrm(ks[17], (DEPTH, D_MODEL), 0.05),
        "ln_b": nrm(ks[18], (DEPTH, D_MODEL), 0.02),
    }


def reference(x_prompt, x_sample, mem_prompt, cache_conv, cache_att_k, cache_att_v, cache_mem_k, cache_mem_v,
              w_in, conv_w, conv_b, conv_ln_g, conv_ln_b, w_pw, rel_table, w_mem_kv, w_out, ln_g, ln_b):
    B, T, _ = x_prompt.shape
    n_mem = mem_prompt.shape[1]
    keep = min(ATT_WIN, T)
    xp, xs = x_prompt, x_sample
    conv_p, kp_l, vp_l, mk_l, mv_l = [], [], [], [], []
    conv_s, ks_l, vs_l = [], [], []
    for l in range(DEPTH):
        mk, mv = jnp.split(mem_prompt @ w_mem_kv[l], 2, axis=-1)
        mk = mk.reshape(B, n_mem, N_MEM_HEADS, HEAD_DIM)
        mv = mv.reshape(B, n_mem, N_MEM_HEADS, HEAD_DIM)
        left = jnp.zeros((B, CONV_W - 1, D_CONV), xp.dtype)
        xp, cs, k, v = layer_step(xp, mk, mv, left, None, None, w_in[l], conv_w[l], conv_b[l], conv_ln_g[l],
                                  conv_ln_b[l], w_pw[l], rel_table[l], w_out[l], ln_g[l], ln_b[l])
        conv_p.append(cs)
        kp_l.append(k[:, T - keep:])
        vp_l.append(v[:, T - keep:])
        mk_l.append(mk)
        mv_l.append(mv)
        xs, cs2, k2, v2 = layer_step(xs, cache_mem_k[l], cache_mem_v[l], cache_conv[l], cache_att_k[l],
                                     cache_att_v[l], w_in[l], conv_w[l], conv_b[l], conv_ln_g[l], conv_ln_b[l],
                                     w_pw[l], rel_table[l], w_out[l], ln_g[l], ln_b[l])
        conv_s.append(cs2)
        ks_l.append(k2)
        vs_l.append(v2)
    return (xp, xs, jnp.stack(conv_p), jnp.stack(kp_l), jnp.stack(vp_l), jnp.stack(mk_l), jnp.stack(mv_l),
            jnp.stack(conv_s), jnp.stack(ks_l), jnp.stack(vs_l))
```

```python
import functools

import jax
import jax.numpy as jnp
from jax import lax
from jax.experimental import pallas as pl
from jax.experimental.pallas import tpu as pltpu

D_MODEL = 2048
CHUNK = 64
N_LEFT_CHUNKS = 8
ATT_WIN = N_LEFT_CHUNKS * CHUNK
BAND = ATT_WIN + CHUNK
CONV_W = 31
CONV_LEFT = CONV_W - 1
D_CONV = D_MODEL // 2
HEAD_DIM = 128
N_HEADS = 4
D_ATT = N_HEADS * HEAD_DIM
N_MEM = 256
REL_CLIP = 128
D_MIX = D_CONV + 2 * D_ATT
D_IN = 3 * D_CONV + 6 * D_ATT
LN_EPS = 1e-5
SM_SCALE = HEAD_DIM ** -0.5
MASK_VALUE = -1e30

COL_Q, COL_K, COL_V, COL_AG, COL_MQ, COL_MG = 6, 7, 8, 9, 10, 11
N_COL_BLOCKS = D_IN // D_ATT

SUBLANES = 8
HALO = 32

BF16 = jnp.bfloat16
F32 = jnp.float32

VMEM_LIMIT_BYTES = 56 * 1024 * 1024


def _compiler_params(n_axes):
    return pltpu.CompilerParams(
        dimension_semantics=("arbitrary",) * n_axes, vmem_limit_bytes=VMEM_LIMIT_BYTES)


def _sigmoid(x):
    return 1.0 / (1.0 + jnp.exp(-x))


def _in_proj_kernel(x_ref, w_ref, h_ref, kv_ref, xb_ref):
    j = pl.program_id(1)

    @pl.when(j == 0)
    def _():
        xb_ref[...] = x_ref[...].astype(BF16)

    acc = jnp.dot(xb_ref[...], w_ref[...], preferred_element_type=F32)
    sg = _sigmoid(acc)
    is_sig = (j == 2) | (j == 3)
    is_silu = (j == 4) | (j == 5) | (j == COL_AG) | (j == COL_MG)
    out = jnp.where(is_sig, sg, jnp.where(is_silu, acc * sg, acc))
    h_ref[...] = out.astype(BF16)

    @pl.when((j == COL_K) | (j == COL_V))
    def _():
        kv_ref[...] = acc


def _in_proj(x, w):
    m = x.shape[0]
    tm = min(1024, m)
    assert m % tm == 0
    return pl.pallas_call(
        _in_proj_kernel,
        grid=(m // tm, N_COL_BLOCKS),
        in_specs=[
            pl.BlockSpec((tm, D_MODEL), lambda i, j: (i, 0)),
            pl.BlockSpec((D_MODEL, D_ATT), lambda i, j: (0, j)),
        ],
        out_specs=[
            pl.BlockSpec((tm, D_ATT), lambda i, j: (i, j)),
            pl.BlockSpec((tm, D_ATT), lambda i, j: (i, jnp.clip(j - COL_K, 0, 1))),
        ],
        out_shape=[
            jax.ShapeDtypeStruct((m, D_IN), BF16),
            jax.ShapeDtypeStruct((m, 2 * D_ATT), F32),
        ],
        scratch_shapes=[pltpu.VMEM((tm, D_MODEL), BF16)],
        compiler_params=_compiler_params(2),
        name="in_proj",
    )(x, w)


def _mem_kv_kernel(x_ref, w_ref, k_ref, v_ref, kb_ref, vb_ref):
    acc = jnp.dot(x_ref[...], w_ref[0], preferred_element_type=F32)
    k = acc[:, :D_ATT]
    v = acc[:, D_ATT:]
    k_ref[0] = k
    v_ref[0] = v
    kb_ref[0] = k.astype(BF16)
    vb_ref[0] = v.astype(BF16)


def _mem_kv(mem, w):
    m = mem.shape[0]
    depth = w.shape[0]
    tm = min(1024, m)
    assert m % tm == 0
    out_spec = pl.BlockSpec((1, tm, D_ATT), lambda i, l: (l, i, 0))
    return pl.pallas_call(
        _mem_kv_kernel,
        grid=(m // tm, depth),
        in_specs=[
            pl.BlockSpec((tm, D_MODEL), lambda i, l: (i, 0)),
            pl.BlockSpec((1, D_MODEL, 2 * D_ATT), lambda i, l: (l, 0, 0)),
        ],
        out_specs=[out_spec] * 4,
        out_shape=[jax.ShapeDtypeStruct((depth, m, D_ATT), F32)] * 2
        + [jax.ShapeDtypeStruct((depth, m, D_ATT), BF16)] * 2,
        compiler_params=_compiler_params(2),
        name="mem_kv",
    )(mem, w)


CONV_ROWS = 32
CONV_LANES = 256


def _conv_kernel(a_ref, sb_ref, g_ref, ah_ref, sbh_ref, ctx_ref, cw_ref, cp_ref, wpw_ref,
                 o_ref, st_ref, up_ref, us_ref, y_ref, *, tq, n_t):
    t = pl.program_id(1)

    up_ref[HALO:HALO + tq, :] = a_ref[...].astype(F32) * sb_ref[...].astype(F32)

    @pl.when(t == 0)
    def _():
        up_ref[0:HALO, :] = ctx_ref[0]

    @pl.when(t > 0)
    def _():
        up_ref[0:HALO, :] = ah_ref[...].astype(F32) * sbh_ref[...].astype(F32)

    n_shifted = tq + HALO - SUBLANES
    for s in range(1, SUBLANES):
        us_ref[s - 1, 0:n_shifted, :] = up_ref[s:s + n_shifted, :]

    for lc in range(D_CONV // CONV_LANES):
        lanes = slice(lc * CONV_LANES, (lc + 1) * CONV_LANES)

        def row_block(rb, carry, lanes=lanes):
            r0 = pl.multiple_of(rb * CONV_ROWS, CONV_ROWS)
            acc = jnp.broadcast_to(cp_ref[0:1, lanes], (CONV_ROWS, CONV_LANES))
            for j in range(CONV_W):
                off = HALO - CONV_LEFT + j
                s = off % SUBLANES
                rows = pl.ds(r0 + (off - s), CONV_ROWS)
                tap = up_ref[rows, lanes] if s == 0 else us_ref[s - 1, rows, lanes]
                acc = acc + tap * cw_ref[j:j + 1, lanes]
            y_ref[pl.ds(r0, CONV_ROWS), lanes] = acc
            return carry

        lax.fori_loop(0, tq // CONV_ROWS, row_block, 0)

    y = y_ref[...]
    mu = jnp.mean(y, axis=-1, keepdims=True)
    d = y - mu
    var = jnp.mean(d * d, axis=-1, keepdims=True)
    yn = d * lax.rsqrt(var + LN_EPS) * cp_ref[1:2, :] + cp_ref[2:3, :]
    sw = yn * _sigmoid(yn)
    pw = jnp.dot(sw.astype(BF16), wpw_ref[...], preferred_element_type=F32)
    o_ref[...] = (pw * g_ref[...].astype(F32)).astype(BF16)

    @pl.when(t == n_t - 1)
    def _():
        st_ref[0] = up_ref[HALO + tq - CONV_LEFT:HALO + tq, :]


def _conv_branch(h, ctx, cw, cp, wpw, *, n_seq, t_len, tq):
    assert t_len % tq == 0 and tq % HALO == 0 and tq >= HALO
    n_t = t_len // tq
    m = n_seq * t_len

    def row_map(col):
        return lambda b, t: (b * n_t + t, col)

    def halo_map(col):
        return lambda b, t: (jnp.maximum((b * t_len + t * tq) // HALO - 1, 0), col)

    return pl.pallas_call(
        functools.partial(_conv_kernel, tq=tq, n_t=n_t),
        grid=(n_seq, n_t),
        in_specs=[
            pl.BlockSpec((tq, D_CONV), row_map(0)),
            pl.BlockSpec((tq, D_CONV), row_map(1)),
            pl.BlockSpec((tq, D_CONV), row_map(2)),
            pl.BlockSpec((HALO, D_CONV), halo_map(0)),
            pl.BlockSpec((HALO, D_CONV), halo_map(1)),
            pl.BlockSpec((1, HALO, D_CONV), lambda b, t: (b, 0, 0)),
            pl.BlockSpec((HALO, D_CONV), lambda b, t: (0, 0)),
            pl.BlockSpec((8, D_CONV), lambda b, t: (0, 0)),
            pl.BlockSpec((D_CONV, D_CONV), lambda b, t: (0, 0)),
        ],
        out_specs=[
            pl.BlockSpec((tq, D_CONV), lambda b, t: (b * n_t + t, 0)),
            pl.BlockSpec((1, CONV_LEFT, D_CONV), lambda b, t: (b, 0, 0)),
        ],
        out_shape=[
            jax.ShapeDtypeStruct((m, D_CONV), BF16),
            jax.ShapeDtypeStruct((n_seq, CONV_LEFT, D_CONV), F32),
        ],
        scratch_shapes=[
            pltpu.VMEM((HALO + tq, D_CONV), F32),
            pltpu.VMEM((SUBLANES - 1, HALO + tq, D_CONV), F32),
            pltpu.VMEM((tq, D_CONV), F32),
        ],
        compiler_params=_compiler_params(2),
        name="conv_branch",
    )(h, h, h, h, h, ctx, cw, cp, wpw)


def _softmax_pv(s, v):
    m = jnp.max(s, axis=-1, keepdims=True)
    p = jnp.exp(s - m)
    l = jnp.sum(p, axis=-1, keepdims=True)
    o = jnp.dot(p.astype(BF16), v, preferred_element_type=F32)
    return o / l


def _qk(q, k):
    return lax.dot_general(q, k, (((1,), (1,)), ((), ())), preferred_element_type=F32)


def _memory_attention(mq_ref, mk_ref, mv_ref, mg_ref, o_ref):
    for h in range(N_HEADS):
        cols = slice(h * HEAD_DIM, (h + 1) * HEAD_DIM)
        s = _qk(mq_ref[:, cols], mk_ref[0, :, cols]) * SM_SCALE
        o = _softmax_pv(s, mv_ref[0, :, cols])
        o_ref[:, D_ATT + h * HEAD_DIM:D_ATT + (h + 1) * HEAD_DIM] = (
            o * mg_ref[:, cols].astype(F32)).astype(BF16)


def _attn_prompt_kernel(q_ref, k_ref, v_ref, ag_ref, mq_ref, mg_ref, mk_ref, mv_ref, bias_ref,
                        o_ref, *, tq):
    t = pl.program_id(1)
    n_c = tq // CHUNK

    def chunk(cc, carry):
        c = t * n_c + cc
        start = pl.multiple_of(jnp.maximum(c * CHUNK - ATT_WIN, 0), CHUNK)
        variant = jnp.minimum(c, N_LEFT_CHUNKS)
        rows = pl.ds(pl.multiple_of(cc * CHUNK, CHUNK), CHUNK)
        for h in range(N_HEADS):
            cols = slice(h * HEAD_DIM, (h + 1) * HEAD_DIM)
            s = _qk(q_ref[rows, cols], k_ref[pl.ds(start, BAND), cols]) * SM_SCALE
            s = s + bias_ref[variant, h]
            o = _softmax_pv(s, v_ref[pl.ds(start, BAND), cols])
            o_ref[rows, cols] = (o * ag_ref[rows, cols].astype(F32)).astype(BF16)
        return carry

    lax.fori_loop(0, n_c, chunk, 0)
    _memory_attention(mq_ref, mk_ref, mv_ref, mg_ref, o_ref)


def _attn_prompt(h, mk, mv, bias, *, n_seq, t_len, tq):
    assert t_len % tq == 0 and tq % CHUNK == 0 and t_len >= BAND
    n_t = t_len // tq

    def row_map(col):
        return lambda b, t: (b * n_t + t, col)

    return pl.pallas_call(
        functools.partial(_attn_prompt_kernel, tq=tq),
        grid=(n_seq, n_t),
        in_specs=[
            pl.BlockSpec((tq, D_ATT), row_map(COL_Q)),
            pl.BlockSpec((t_len, D_ATT), lambda b, t: (b, COL_K)),
            pl.BlockSpec((t_len, D_ATT), lambda b, t: (b, COL_V)),
            pl.BlockSpec((tq, D_ATT), row_map(COL_AG)),
            pl.BlockSpec((tq, D_ATT), row_map(COL_MQ)),
            pl.BlockSpec((tq, D_ATT), row_map(COL_MG)),
            pl.BlockSpec((1, N_MEM, D_ATT), lambda b, t: (b, 0, 0)),
            pl.BlockSpec((1, N_MEM, D_ATT), lambda b, t: (b, 0, 0)),
            pl.BlockSpec(bias.shape, lambda b, t: (0, 0, 0, 0)),
        ],
        out_specs=pl.BlockSpec((tq, 2 * D_ATT), lambda b, t: (b * n_t + t, 0)),
        out_shape=jax.ShapeDtypeStruct((n_seq * t_len, 2 * D_ATT), BF16),
        compiler_params=_compiler_params(2),
        name="attn_prompt",
    )(h, h, h, h, h, h, mk, mv, bias)


def _attn_sample_kernel(q_ref, k_ref, v_ref, ag_ref, mq_ref, mg_ref, kc_ref, vc_ref, mk_ref, mv_ref,
                        bias_ref, o_ref):
    for h in range(N_HEADS):
        cols = slice(h * HEAD_DIM, (h + 1) * HEAD_DIM)
        kk = jnp.concatenate([kc_ref[0, :, cols], k_ref[:, cols]], axis=0)
        vv = jnp.concatenate([vc_ref[0, :, cols], v_ref[:, cols]], axis=0)
        s = _qk(q_ref[:, cols], kk) * SM_SCALE + bias_ref[N_LEFT_CHUNKS, h]
        o = _softmax_pv(s, vv)
        o_ref[:, cols] = (o * ag_ref[:, cols].astype(F32)).astype(BF16)
    _memory_attention(mq_ref, mk_ref, mv_ref, mg_ref, o_ref)


def _attn_sample(h, kc, vc, mk, mv, bias, *, n_seq):
    def row_map(col):
        return lambda b: (b, col)

    seq_spec = lambda n: pl.BlockSpec((1, n, D_ATT), lambda b: (b, 0, 0))
    return pl.pallas_call(
        _attn_sample_kernel,
        grid=(n_seq,),
        in_specs=[
            pl.BlockSpec((CHUNK, D_ATT), row_map(COL_Q)),
            pl.BlockSpec((CHUNK, D_ATT), row_map(COL_K)),
            pl.BlockSpec((CHUNK, D_ATT), row_map(COL_V)),
            pl.BlockSpec((CHUNK, D_ATT), row_map(COL_AG)),
            pl.BlockSpec((CHUNK, D_ATT), row_map(COL_MQ)),
            pl.BlockSpec((CHUNK, D_ATT), row_map(COL_MG)),
            seq_spec(ATT_WIN), seq_spec(ATT_WIN), seq_spec(N_MEM), seq_spec(N_MEM),
            pl.BlockSpec(bias.shape, lambda b: (0, 0, 0, 0)),
        ],
        out_specs=pl.BlockSpec((CHUNK, 2 * D_ATT), lambda b: (b, 0)),
        out_shape=jax.ShapeDtypeStruct((n_seq * CHUNK, 2 * D_ATT), BF16),
        compiler_params=_compiler_params(1),
        name="attn_sample",
    )(h, h, h, h, h, h, kc, vc, mk, mv, bias)


OUT_COLS = 512


def _out_proj_kernel(cm_ref, am_ref, x_ref, w_ref, gb_ref, y_ref, z_ref, *, alpha):
    tm = x_ref.shape[0]
    row_sum = jnp.zeros((tm, 1), F32)
    for n in range(D_MODEL // OUT_COLS):
        cols = slice(n * OUT_COLS, (n + 1) * OUT_COLS)
        z = jnp.dot(cm_ref[...], w_ref[0:D_CONV, cols], preferred_element_type=F32)
        z = z + jnp.dot(am_ref[...], w_ref[D_CONV:D_MIX, cols], preferred_element_type=F32)
        z = alpha * x_ref[:, cols] + z
        z_ref[:, cols] = z
        row_sum = row_sum + jnp.sum(z, axis=-1, keepdims=True)
    mu = row_sum * (1.0 / D_MODEL)
    sq_sum = jnp.zeros((tm, 1), F32)
    for n in range(D_MODEL // OUT_COLS):
        cols = slice(n * OUT_COLS, (n + 1) * OUT_COLS)
        d = z_ref[:, cols] - mu
        sq_sum = sq_sum + jnp.sum(d * d, axis=-1, keepdims=True)
    inv = lax.rsqrt(sq_sum * (1.0 / D_MODEL) + LN_EPS)
    for n in range(D_MODEL // OUT_COLS):
        cols = slice(n * OUT_COLS, (n + 1) * OUT_COLS)
        y_ref[:, cols] = (z_ref[:, cols] - mu) * inv * gb_ref[0:1, cols] + gb_ref[1:2, cols]


def _out_proj(cm, am, x, w, gb, *, alpha):
    m = x.shape[0]
    tm = min(512, m)
    assert m % tm == 0
    return pl.pallas_call(
        functools.partial(_out_proj_kernel, alpha=alpha),
        grid=(m // tm,),
        in_specs=[
            pl.BlockSpec((tm, D_CONV), lambda i: (i, 0)),
            pl.BlockSpec((tm, 2 * D_ATT), lambda i: (i, 0)),
            pl.BlockSpec((tm, D_MODEL), lambda i: (i, 0)),
            pl.BlockSpec((D_MIX, D_MODEL), lambda i: (0, 0)),
            pl.BlockSpec((8, D_MODEL), lambda i: (0, 0)),
        ],
        out_specs=pl.BlockSpec((tm, D_MODEL), lambda i: (i, 0)),
        out_shape=jax.ShapeDtypeStruct((m, D_MODEL), F32),
        scratch_shapes=[pltpu.VMEM((tm, D_MODEL), F32)],
        compiler_params=_compiler_params(1),
        name="out_proj",
    )(cm, am, x, w, gb)


def _band_bias(table):
    q_pos = jnp.arange(CHUNK) + ATT_WIN
    k_pos = jnp.arange(BAND)
    d = jnp.clip(q_pos[:, None] - k_pos[None, :], -REL_CLIP, REL_CLIP) + REL_CLIP
    full = table[:, d].astype(F32)
    variants = []
    for c in range(N_LEFT_CHUNKS):
        off = (N_LEFT_CHUNKS - c) * CHUNK
        pad = jnp.full((table.shape[0], CHUNK, off), MASK_VALUE, F32)
        variants.append(jnp.concatenate([full[:, :, off:], pad], axis=-1))
    variants.append(full)
    return jnp.stack(variants)


def _pack_rows(rows, width):
    out = jnp.zeros((8, width), F32)
    for i, r in enumerate(rows):
        out = out.at[i].set(r.astype(F32))
    return out


def kernel(x_prompt, x_sample, mem_prompt, cache_conv, cache_att_k, cache_att_v, cache_mem_k, cache_mem_v,
           w_in, conv_w, conv_b, conv_ln_g, conv_ln_b, w_pw, rel_table, w_mem_kv, w_out, ln_g, ln_b):
    n_p, t_p, _ = x_prompt.shape
    n_s, t_s, _ = x_sample.shape
    depth = w_in.shape[0]
    n_mem = mem_prompt.shape[1]
    assert t_s == CHUNK and n_mem == N_MEM and cache_att_k.shape[2] == ATT_WIN
    alpha = (2 * depth) ** 0.25
    keep = min(ATT_WIN, t_p)

    w_in_b = w_in.astype(BF16)
    w_pw_b = w_pw.astype(BF16)
    w_out_b = w_out.astype(BF16)
    w_mkv_b = w_mem_kv.astype(BF16)

    mk_p, mv_p, mk_pb, mv_pb = _mem_kv(mem_prompt.reshape(n_p * n_mem, D_MODEL).astype(BF16), w_mkv_b)

    xp = x_prompt.reshape(n_p * t_p, D_MODEL)
    xs = x_sample.reshape(n_s * t_s, D_MODEL)
    zero_ctx = jnp.zeros((n_p, HALO, D_CONV), F32)
    cache_ctx = jnp.pad(cache_conv, ((0, 0), (0, 0), (HALO - CONV_LEFT, 0), (0, 0)))

    conv_p, kp_l, vp_l, conv_s, ks_l, vs_l = [], [], [], [], [], []
    for l in range(depth):
        cw = jnp.pad(conv_w[l], ((0, HALO - CONV_W), (0, 0)))
        cp = _pack_rows([conv_b[l], conv_ln_g[l], conv_ln_b[l]], D_CONV)
        gb = _pack_rows([ln_g[l], ln_b[l]], D_MODEL)
        bias = _band_bias(rel_table[l])

        h, kv = _in_proj(xp, w_in_b[l])
        cm, cs = _conv_branch(h, zero_ctx, cw, cp, w_pw_b[l], n_seq=n_p, t_len=t_p, tq=256)
        am = _attn_prompt(h, mk_pb[l].reshape(n_p, n_mem, D_ATT), mv_pb[l].reshape(n_p, n_mem, D_ATT),
                          bias, n_seq=n_p, t_len=t_p, tq=512)
        xp = _out_proj(cm, am, xp, w_out_b[l], gb, alpha=alpha)
        conv_p.append(cs)
        kv = kv.reshape(n_p, t_p, 2, N_HEADS, HEAD_DIM)
        kp_l.append(kv[:, t_p - keep:, 0])
        vp_l.append(kv[:, t_p - keep:, 1])

        h, kv = _in_proj(xs, w_in_b[l])
        cm, cs = _conv_branch(h, cache_ctx[l], cw, cp, w_pw_b[l], n_seq=n_s, t_len=t_s, tq=t_s)
        am = _attn_sample(h, cache_att_k[l].reshape(n_s, ATT_WIN, D_ATT).astype(BF16),
                          cache_att_v[l].reshape(n_s, ATT_WIN, D_ATT).astype(BF16),
                          cache_mem_k[l].reshape(n_s, n_mem, D_ATT).astype(BF16),
                          cache_mem_v[l].reshape(n_s, n_mem, D_ATT).astype(BF16), bias, n_seq=n_s)
        xs = _out_proj(cm, am, xs, w_out_b[l], gb, alpha=alpha)
        conv_s.append(cs)
        kv = kv.reshape(n_s, t_s, 2, N_HEADS, HEAD_DIM)
        ks_l.append(kv[:, :, 0])
        vs_l.append(kv[:, :, 1])

    mem_shape = (depth, n_p, n_mem, N_HEADS, HEAD_DIM)
    return (xp.reshape(n_p, t_p, D_MODEL), xs.reshape(n_s, t_s, D_MODEL), jnp.stack(conv_p),
            jnp.stack(kp_l), jnp.stack(vp_l), mk_p.reshape(mem_shape), mv_p.reshape(mem_shape),
            jnp.stack(conv_s), jnp.stack(ks_l), jnp.stack(vs_l))
```

```python
import functools

import jax
import jax.numpy as jnp
from jax import lax
from jax.experimental import pallas as pl
from jax.experimental.pallas import tpu as pltpu

D_MODEL = 2048
CHUNK = 64
N_LEFT_CHUNKS = 8
ATT_WIN = N_LEFT_CHUNKS * CHUNK
BAND = ATT_WIN + CHUNK
CONV_W = 31
CONV_LEFT = CONV_W - 1
D_CONV = D_MODEL // 2
HEAD_DIM = 128
N_HEADS = 4
D_ATT = N_HEADS * HEAD_DIM
N_MEM = 256
REL_CLIP = 128
D_MIX = D_CONV + 2 * D_ATT
D_IN = 3 * D_CONV + 6 * D_ATT
LN_EPS = 1e-5
SM_SCALE = HEAD_DIM ** -0.5
MASK_VALUE = -1e30

COL_Q, COL_K, COL_V, COL_AG, COL_MQ, COL_MG = 6, 7, 8, 9, 10, 11
N_COL_BLOCKS = D_IN // D_ATT

SUBLANES = 8
HALO = 32

Q_CHUNKS = 2
Q_TILE = Q_CHUNKS * CHUNK
Q_BAND = ATT_WIN + Q_TILE
N_BIAS_VARIANTS = N_LEFT_CHUNKS // Q_CHUNKS + 1

BF16 = jnp.bfloat16
F32 = jnp.float32

VMEM_LIMIT_BYTES = 56 * 1024 * 1024


def _compiler_params(n_axes):
    return pltpu.CompilerParams(
        dimension_semantics=("arbitrary",) * n_axes, vmem_limit_bytes=VMEM_LIMIT_BYTES)


def _sigmoid(x):
    return 1.0 / (1.0 + jnp.exp(-x))


def _in_proj_kernel(x_ref, w_ref, h_ref, k_ref, v_ref, xb_ref):
    j = pl.program_id(1)
    keep = k_ref.shape[0]

    @pl.when(j == 0)
    def _():
        xb_ref[...] = x_ref[...].astype(BF16)

    acc = jnp.dot(xb_ref[...], w_ref[...], preferred_element_type=F32)
    sg = _sigmoid(acc)
    is_sig = (j == 2) | (j == 3)
    is_silu = (j == 4) | (j == 5) | (j == COL_AG) | (j == COL_MG)
    out = jnp.where(is_sig, sg, jnp.where(is_silu, acc * sg, acc))
    h_ref[...] = out.astype(BF16)

    @pl.when(j == COL_K)
    def _():
        k_ref[...] = acc[acc.shape[0] - keep:, :]

    @pl.when(j == COL_V)
    def _():
        v_ref[...] = acc[acc.shape[0] - keep:, :]


def _in_proj(x, w, *, seq_rows, keep):
    m = x.shape[0]
    tm = min(1024, m)
    assert m % tm == 0 and seq_rows % tm == 0 and keep <= tm
    tiles_per_seq = seq_rows // tm
    n_seq = m // seq_rows
    keep_spec = pl.BlockSpec((keep, D_ATT), lambda i, j: (i // tiles_per_seq, 0))
    return pl.pallas_call(
        _in_proj_kernel,
        grid=(m // tm, N_COL_BLOCKS),
        in_specs=[
            pl.BlockSpec((tm, D_MODEL), lambda i, j: (i, 0)),
            pl.BlockSpec((D_MODEL, D_ATT), lambda i, j: (0, j)),
        ],
        out_specs=[pl.BlockSpec((tm, D_ATT), lambda i, j: (i, j)), keep_spec, keep_spec],
        out_shape=[
            jax.ShapeDtypeStruct((m, D_IN), BF16),
            jax.ShapeDtypeStruct((n_seq * keep, D_ATT), F32),
            jax.ShapeDtypeStruct((n_seq * keep, D_ATT), F32),
        ],
        scratch_shapes=[pltpu.VMEM((tm, D_MODEL), BF16)],
        compiler_params=_compiler_params(2),
        name="in_proj",
    )(x, w)


def _mem_kv_kernel(x_ref, w_ref, k_ref, v_ref, kb_ref, vb_ref):
    acc = jnp.dot(x_ref[...], w_ref[0], preferred_element_type=F32)
    k = acc[:, :D_ATT]
    v = acc[:, D_ATT:]
    k_ref[0] = k
    v_ref[0] = v
    kb_ref[0] = k.astype(BF16)
    vb_ref[0] = v.astype(BF16)


def _mem_kv(mem, w):
    m = mem.shape[0]
    depth = w.shape[0]
    tm = min(1024, m)
    assert m % tm == 0
    out_spec = pl.BlockSpec((1, tm, D_ATT), lambda i, l: (l, i, 0))
    return pl.pallas_call(
        _mem_kv_kernel,
        grid=(m // tm, depth),
        in_specs=[
            pl.BlockSpec((tm, D_MODEL), lambda i, l: (i, 0)),
            pl.BlockSpec((1, D_MODEL, 2 * D_ATT), lambda i, l: (l, 0, 0)),
        ],
        out_specs=[out_spec] * 4,
        out_shape=[jax.ShapeDtypeStruct((depth, m, D_ATT), F32)] * 2
        + [jax.ShapeDtypeStruct((depth, m, D_ATT), BF16)] * 2,
        compiler_params=_compiler_params(2),
        name="mem_kv",
    )(mem, w)


CONV_ROWS = 32
CONV_LANES = 256


def _conv_kernel(a_ref, sb_ref, g_ref, ah_ref, sbh_ref, ctx_ref, cw_ref, cp_ref, wpw_ref,
                 o_ref, st_ref, up_ref, us_ref, y_ref, *, tq, n_t):
    t = pl.program_id(1)

    up_ref[HALO:HALO + tq, :] = a_ref[...].astype(F32) * sb_ref[...].astype(F32)

    @pl.when(t == 0)
    def _():
        up_ref[0:HALO, :] = ctx_ref[0]

    @pl.when(t > 0)
    def _():
        up_ref[0:HALO, :] = ah_ref[...].astype(F32) * sbh_ref[...].astype(F32)

    n_shifted = tq + HALO - SUBLANES
    for s in range(1, SUBLANES):
        us_ref[s - 1, 0:n_shifted, :] = up_ref[s:s + n_shifted, :]

    groups = CONV_ROWS // SUBLANES
    for lc in range(D_CONV // CONV_LANES):
        lanes = slice(lc * CONV_LANES, (lc + 1) * CONV_LANES)

        def row_block(rb, carry, lanes=lanes):
            r0 = pl.multiple_of(rb * CONV_ROWS, CONV_ROWS)
            acc = jnp.zeros((groups, SUBLANES, CONV_LANES), F32)
            for s in range(SUBLANES):
                offs = [o for o in range(HALO - CONV_LEFT, HALO + 1) if o % SUBLANES == s]
                lo = offs[0] - s
                n_win = offs[-1] - s + CONV_ROWS - lo
                rows = pl.ds(r0 + lo, n_win)
                win = up_ref[rows, lanes] if s == 0 else us_ref[s - 1, rows, lanes]
                for o in offs:
                    tap = win[o - s - lo:o - s - lo + CONV_ROWS]
                    w = cw_ref[o - (HALO - CONV_LEFT), :, lanes]
                    acc = acc + tap.reshape(groups, SUBLANES, CONV_LANES) * w[None]
            y_ref[pl.ds(r0, CONV_ROWS), lanes] = acc.reshape(CONV_ROWS, CONV_LANES) + cp_ref[0:1, lanes]
            return carry

        lax.fori_loop(0, tq // CONV_ROWS, row_block, 0)

    y = y_ref[...]
    mu = jnp.mean(y, axis=-1, keepdims=True)
    d = y - mu
    var = jnp.mean(d * d, axis=-1, keepdims=True)
    yn = d * lax.rsqrt(var + LN_EPS) * cp_ref[1:2, :] + cp_ref[2:3, :]
    sw = yn * _sigmoid(yn)
    pw = jnp.dot(sw.astype(BF16), wpw_ref[...], preferred_element_type=F32)
    o_ref[...] = (pw * g_ref[...].astype(F32)).astype(BF16)

    @pl.when(t == n_t - 1)
    def _():
        st_ref[0] = up_ref[HALO + tq - CONV_LEFT:HALO + tq, :]


def _conv_branch(h, ctx, cw, cp, wpw, *, n_seq, t_len, tq):
    assert t_len % tq == 0 and tq % HALO == 0 and tq >= HALO
    n_t = t_len // tq
    m = n_seq * t_len

    def row_map(col):
        return lambda b, t: (b * n_t + t, col)

    def halo_map(col):
        return lambda b, t: (jnp.maximum((b * t_len + t * tq) // HALO - 1, 0), col)

    return pl.pallas_call(
        functools.partial(_conv_kernel, tq=tq, n_t=n_t),
        grid=(n_seq, n_t),
        in_specs=[
            pl.BlockSpec((tq, D_CONV), row_map(0)),
            pl.BlockSpec((tq, D_CONV), row_map(1)),
            pl.BlockSpec((tq, D_CONV), row_map(2)),
            pl.BlockSpec((HALO, D_CONV), halo_map(0)),
            pl.BlockSpec((HALO, D_CONV), halo_map(1)),
            pl.BlockSpec((1, HALO, D_CONV), lambda b, t: (b, 0, 0)),
            pl.BlockSpec((CONV_W, SUBLANES, D_CONV), lambda b, t: (0, 0, 0)),
            pl.BlockSpec((8, D_CONV), lambda b, t: (0, 0)),
            pl.BlockSpec((D_CONV, D_CONV), lambda b, t: (0, 0)),
        ],
        out_specs=[
            pl.BlockSpec((tq, D_CONV), lambda b, t: (b * n_t + t, 0)),
            pl.BlockSpec((1, CONV_LEFT, D_CONV), lambda b, t: (b, 0, 0)),
        ],
        out_shape=[
            jax.ShapeDtypeStruct((m, D_CONV), BF16),
            jax.ShapeDtypeStruct((n_seq, CONV_LEFT, D_CONV), F32),
        ],
        scratch_shapes=[
            pltpu.VMEM((HALO + tq, D_CONV), F32),
            pltpu.VMEM((SUBLANES - 1, HALO + tq, D_CONV), F32),
            pltpu.VMEM((tq, D_CONV), F32),
        ],
        compiler_params=_compiler_params(2),
        name="conv_branch",
    )(h, h, h, h, h, ctx, cw, cp, wpw)


def _qk(q, k):
    return lax.dot_general(q, k, (((1,), (1,)), ((), ())), preferred_element_type=F32)


def _attend(qs, ks, vs, biases):
    ss = [_qk(q, k) * SM_SCALE for q, k in zip(qs, ks)]
    ss = [s if b is None else s + b for s, b in zip(ss, biases)]
    ps, ls = [], []
    for s in ss:
        m = jnp.max(s, axis=-1, keepdims=True)
        p = jnp.exp(s - m)
        ls.append(jnp.sum(p, axis=-1, keepdims=True))
        ps.append(p.astype(BF16))
    return [jnp.dot(p, v, preferred_element_type=F32) / l for p, v, l in zip(ps, vs, ls)]


def _head_cols(h):
    return slice(h * HEAD_DIM, (h + 1) * HEAD_DIM)


def _memory_attention(mq_ref, mk_ref, mv_ref, mg_ref, o_ref):
    heads = range(N_HEADS)
    os = _attend([mq_ref[:, _head_cols(h)] for h in heads], [mk_ref[0, :, _head_cols(h)] for h in heads],
                 [mv_ref[0, :, _head_cols(h)] for h in heads], [None] * N_HEADS)
    for h in heads:
        o_ref[:, D_ATT + h * HEAD_DIM:D_ATT + (h + 1) * HEAD_DIM] = (
            os[h] * mg_ref[:, _head_cols(h)].astype(F32)).astype(BF16)


def _attn_prompt_kernel(q_ref, k_ref, v_ref, ag_ref, mq_ref, mg_ref, mk_ref, mv_ref, bias_ref,
                        o_ref, *, tq):
    t = pl.program_id(1)
    n_q = tq // Q_TILE
    heads = range(N_HEADS)

    def q_tile(qi, carry):
        c0 = (t * n_q + qi) * Q_CHUNKS
        start = pl.multiple_of(jnp.maximum(c0 * CHUNK - ATT_WIN, 0), Q_TILE)
        variant = jnp.minimum(c0 // Q_CHUNKS, N_BIAS_VARIANTS - 1)
        rows = pl.ds(pl.multiple_of(qi * Q_TILE, Q_TILE), Q_TILE)
        band = pl.ds(start, Q_BAND)
        os = _attend([q_ref[rows, _head_cols(h)] for h in heads], [k_ref[band, _head_cols(h)] for h in heads],
                     [v_ref[band, _head_cols(h)] for h in heads], [bias_ref[variant, h] for h in heads])
        for h in heads:
            o_ref[rows, _head_cols(h)] = (os[h] * ag_ref[rows, _head_cols(h)].astype(F32)).astype(BF16)
        return carry

    lax.fori_loop(0, n_q, q_tile, 0)
    _memory_attention(mq_ref, mk_ref, mv_ref, mg_ref, o_ref)


def _attn_prompt(h, mk, mv, bias, *, n_seq, t_len, tq):
    assert t_len % tq == 0 and tq % Q_TILE == 0 and t_len >= Q_BAND
    n_t = t_len // tq

    def row_map(col):
        return lambda b, t: (b * n_t + t, col)

    return pl.pallas_call(
        functools.partial(_attn_prompt_kernel, tq=tq),
        grid=(n_seq, n_t),
        in_specs=[
            pl.BlockSpec((tq, D_ATT), row_map(COL_Q)),
            pl.BlockSpec((t_len, D_ATT), lambda b, t: (b, COL_K)),
            pl.BlockSpec((t_len, D_ATT), lambda b, t: (b, COL_V)),
            pl.BlockSpec((tq, D_ATT), row_map(COL_AG)),
            pl.BlockSpec((tq, D_ATT), row_map(COL_MQ)),
            pl.BlockSpec((tq, D_ATT), row_map(COL_MG)),
            pl.BlockSpec((1, N_MEM, D_ATT), lambda b, t: (b, 0, 0)),
            pl.BlockSpec((1, N_MEM, D_ATT), lambda b, t: (b, 0, 0)),
            pl.BlockSpec(bias.shape, lambda b, t: (0, 0, 0, 0)),
        ],
        out_specs=pl.BlockSpec((tq, 2 * D_ATT), lambda b, t: (b * n_t + t, 0)),
        out_shape=jax.ShapeDtypeStruct((n_seq * t_len, 2 * D_ATT), BF16),
        compiler_params=_compiler_params(2),
        name="attn_prompt",
    )(h, h, h, h, h, h, mk, mv, bias)


def _attn_sample_kernel(q_ref, k_ref, v_ref, ag_ref, mq_ref, mg_ref, kc_ref, vc_ref, mk_ref, mv_ref,
                        bias_ref, o_ref):
    heads = range(N_HEADS)
    ks = [jnp.concatenate([kc_ref[0, :, _head_cols(h)], k_ref[:, _head_cols(h)]], axis=0) for h in heads]
    vs = [jnp.concatenate([vc_ref[0, :, _head_cols(h)], v_ref[:, _head_cols(h)]], axis=0) for h in heads]
    os = _attend([q_ref[:, _head_cols(h)] for h in heads], ks, vs, [bias_ref[h] for h in heads])
    for h in heads:
        o_ref[:, _head_cols(h)] = (os[h] * ag_ref[:, _head_cols(h)].astype(F32)).astype(BF16)
    _memory_attention(mq_ref, mk_ref, mv_ref, mg_ref, o_ref)


def _attn_sample(h, kc, vc, mk, mv, bias, *, n_seq):
    def row_map(col):
        return lambda b: (b, col)

    seq_spec = lambda n: pl.BlockSpec((1, n, D_ATT), lambda b: (b, 0, 0))
    return pl.pallas_call(
        _attn_sample_kernel,
        grid=(n_seq,),
        in_specs=[
            pl.BlockSpec((CHUNK, D_ATT), row_map(COL_Q)),
            pl.BlockSpec((CHUNK, D_ATT), row_map(COL_K)),
            pl.BlockSpec((CHUNK, D_ATT), row_map(COL_V)),
            pl.BlockSpec((CHUNK, D_ATT), row_map(COL_AG)),
            pl.BlockSpec((CHUNK, D_ATT), row_map(COL_MQ)),
            pl.BlockSpec((CHUNK, D_ATT), row_map(COL_MG)),
            seq_spec(ATT_WIN), seq_spec(ATT_WIN), seq_spec(N_MEM), seq_spec(N_MEM),
            pl.BlockSpec(bias.shape, lambda b: (0, 0, 0)),
        ],
        out_specs=pl.BlockSpec((CHUNK, 2 * D_ATT), lambda b: (b, 0)),
        out_shape=jax.ShapeDtypeStruct((n_seq * CHUNK, 2 * D_ATT), BF16),
        compiler_params=_compiler_params(1),
        name="attn_sample",
    )(h, h, h, h, h, h, kc, vc, mk, mv, bias)


OUT_COLS = 512


def _out_proj_kernel(cm_ref, am_ref, x_ref, w_ref, gb_ref, y_ref, z_ref, *, alpha):
    tm = x_ref.shape[0]
    row_sum = jnp.zeros((tm, 1), F32)
    for n in range(D_MODEL // OUT_COLS):
        cols = slice(n * OUT_COLS, (n + 1) * OUT_COLS)
        z = jnp.dot(cm_ref[...], w_ref[0:D_CONV, cols], preferred_element_type=F32)
        z = z + jnp.dot(am_ref[...], w_ref[D_CONV:D_MIX, cols], preferred_element_type=F32)
        z = alpha * x_ref[:, cols] + z
        z_ref[:, cols] = z
        row_sum = row_sum + jnp.sum(z, axis=-1, keepdims=True)
    mu = row_sum * (1.0 / D_MODEL)
    sq_sum = jnp.zeros((tm, 1), F32)
    for n in range(D_MODEL // OUT_COLS):
        cols = slice(n * OUT_COLS, (n + 1) * OUT_COLS)
        d = z_ref[:, cols] - mu
        sq_sum = sq_sum + jnp.sum(d * d, axis=-1, keepdims=True)
    inv = lax.rsqrt(sq_sum * (1.0 / D_MODEL) + LN_EPS)
    for n in range(D_MODEL // OUT_COLS):
        cols = slice(n * OUT_COLS, (n + 1) * OUT_COLS)
        y_ref[:, cols] = (z_ref[:, cols] - mu) * inv * gb_ref[0:1, cols] + gb_ref[1:2, cols]


def _out_proj(cm, am, x, w, gb, *, alpha):
    m = x.shape[0]
    tm = min(512, m)
    assert m % tm == 0
    return pl.pallas_call(
        functools.partial(_out_proj_kernel, alpha=alpha),
        grid=(m // tm,),
        in_specs=[
            pl.BlockSpec((tm, D_CONV), lambda i: (i, 0)),
            pl.BlockSpec((tm, 2 * D_ATT), lambda i: (i, 0)),
            pl.BlockSpec((tm, D_MODEL), lambda i: (i, 0)),
            pl.BlockSpec((D_MIX, D_MODEL), lambda i: (0, 0)),
            pl.BlockSpec((8, D_MODEL), lambda i: (0, 0)),
        ],
        out_specs=pl.BlockSpec((tm, D_MODEL), lambda i: (i, 0)),
        out_shape=jax.ShapeDtypeStruct((m, D_MODEL), F32),
        scratch_shapes=[pltpu.VMEM((tm, D_MODEL), F32)],
        compiler_params=_compiler_params(1),
        name="out_proj",
    )(cm, am, x, w, gb)


def _band_bias(table):
    n_h = table.shape[0]
    n_u = Q_TILE + Q_BAND - 1
    n_flat = n_u - 2 * REL_CLIP
    u = jnp.concatenate([jnp.broadcast_to(table[:, 2 * REL_CLIP:], (n_h, n_flat)),
                         jnp.flip(table[:, 1:], axis=1)], axis=1).astype(F32)
    u = jnp.pad(u, ((0, 0), (0, 1)))
    skew = jnp.broadcast_to(u[:, None, :], (n_h, Q_TILE, n_u + 1)).reshape(n_h, Q_TILE * (n_u + 1))
    full = skew[:, Q_TILE - 1:Q_TILE - 1 + Q_TILE * n_u].reshape(n_h, Q_TILE, n_u)[:, :, :Q_BAND]
    qc = jnp.arange(Q_TILE)[:, None] // CHUNK
    kc = jnp.arange(Q_BAND)[None, :] // CHUNK
    full = jnp.where((kc >= qc) & (kc <= qc + N_LEFT_CHUNKS), full, MASK_VALUE)
    variants = []
    for v in range(N_BIAS_VARIANTS - 1):
        off = ATT_WIN - v * Q_TILE
        pad = jnp.full((n_h, Q_TILE, off), MASK_VALUE, F32)
        variants.append(jnp.concatenate([full[:, :, off:], pad], axis=-1))
    variants.append(full)
    return jnp.stack(variants)


def _pack_rows(rows, width):
    return jnp.concatenate([jnp.stack(rows).astype(F32), jnp.zeros((8 - len(rows), width), F32)])


def kernel(x_prompt, x_sample, mem_prompt, cache_conv, cache_att_k, cache_att_v, cache_mem_k, cache_mem_v,
           w_in, conv_w, conv_b, conv_ln_g, conv_ln_b, w_pw, rel_table, w_mem_kv, w_out, ln_g, ln_b):
    n_p, t_p, _ = x_prompt.shape
    n_s, t_s, _ = x_sample.shape
    depth = w_in.shape[0]
    n_mem = mem_prompt.shape[1]
    assert t_s == CHUNK and n_mem == N_MEM and cache_att_k.shape[2] == ATT_WIN
    alpha = (2 * depth) ** 0.25
    keep = min(ATT_WIN, t_p)

    w_in_b = w_in.astype(BF16)
    w_pw_b = w_pw.astype(BF16)
    w_out_b = w_out.astype(BF16)
    w_mkv_b = w_mem_kv.astype(BF16)

    mk_p, mv_p, mk_pb, mv_pb = _mem_kv(mem_prompt.reshape(n_p * n_mem, D_MODEL).astype(BF16), w_mkv_b)
    mk_pb = mk_pb.reshape(depth, n_p, n_mem, D_ATT)
    mv_pb = mv_pb.reshape(depth, n_p, n_mem, D_ATT)

    xp = x_prompt.reshape(n_p * t_p, D_MODEL)
    xs = x_sample.reshape(n_s * t_s, D_MODEL)
    zero_ctx = jnp.zeros((n_p, HALO, D_CONV), F32)
    cache_ctx = jnp.pad(cache_conv, ((0, 0), (0, 0), (HALO - CONV_LEFT, 0), (0, 0)))
    kc_b = cache_att_k.reshape(depth, n_s, ATT_WIN, D_ATT).astype(BF16)
    vc_b = cache_att_v.reshape(depth, n_s, ATT_WIN, D_ATT).astype(BF16)
    mk_sb = cache_mem_k.reshape(depth, n_s, n_mem, D_ATT).astype(BF16)
    mv_sb = cache_mem_v.reshape(depth, n_s, n_mem, D_ATT).astype(BF16)

    conv_p, kp_l, vp_l, conv_s, ks_l, vs_l = [], [], [], [], [], []
    for l in range(depth):
        cw = jnp.broadcast_to(conv_w[l][:, None, :], (CONV_W, SUBLANES, D_CONV))
        cp = _pack_rows([conv_b[l], conv_ln_g[l], conv_ln_b[l]], D_CONV)
        gb = _pack_rows([ln_g[l], ln_b[l]], D_MODEL)
        bias = _band_bias(rel_table[l])

        h, k_new, v_new = _in_proj(xp, w_in_b[l], seq_rows=t_p, keep=keep)
        cm, cs = _conv_branch(h, zero_ctx, cw, cp, w_pw_b[l], n_seq=n_p, t_len=t_p, tq=256)
        am = _attn_prompt(h, mk_pb[l], mv_pb[l], bias, n_seq=n_p, t_len=t_p, tq=512)
        xp = _out_proj(cm, am, xp, w_out_b[l], gb, alpha=alpha)
        conv_p.append(cs)
        kp_l.append(k_new)
        vp_l.append(v_new)

        h, k_new, v_new = _in_proj(xs, w_in_b[l], seq_rows=n_s * t_s, keep=n_s * t_s)
        cm, cs = _conv_branch(h, cache_ctx[l], cw, cp, w_pw_b[l], n_seq=n_s, t_len=t_s, tq=t_s)
        am = _attn_sample(h, kc_b[l], vc_b[l], mk_sb[l], mv_sb[l], bias[-1, :, :CHUNK, :BAND], n_seq=n_s)
        xs = _out_proj(cm, am, xs, w_out_b[l], gb, alpha=alpha)
        conv_s.append(cs)
        ks_l.append(k_new)
        vs_l.append(v_new)

    mem_shape = (depth, n_p, n_mem, N_HEADS, HEAD_DIM)
    kv_p_shape = (depth, n_p, keep, N_HEADS, HEAD_DIM)
    kv_s_shape = (depth, n_s, t_s, N_HEADS, HEAD_DIM)
    return (xp.reshape(n_p, t_p, D_MODEL), xs.reshape(n_s, t_s, D_MODEL), jnp.stack(conv_p),
            jnp.stack(kp_l).reshape(kv_p_shape), jnp.stack(vp_l).reshape(kv_p_shape),
            mk_p.reshape(mem_shape), mv_p.reshape(mem_shape),
            jnp.stack(conv_s), jnp.stack(ks_l).reshape(kv_s_shape), jnp.stack(vs_l).reshape(kv_s_shape))
```

```python
import functools

import jax
import jax.numpy as jnp
from jax import lax
from jax.experimental import pallas as pl
from jax.experimental.pallas import tpu as pltpu

D_MODEL = 2048
CHUNK = 64
N_LEFT_CHUNKS = 8
ATT_WIN = N_LEFT_CHUNKS * CHUNK
BAND = ATT_WIN + CHUNK
CONV_W = 31
CONV_LEFT = CONV_W - 1
D_CONV = D_MODEL // 2
HEAD_DIM = 128
N_HEADS = 4
D_ATT = N_HEADS * HEAD_DIM
N_MEM = 256
REL_CLIP = 128
D_MIX = D_CONV + 2 * D_ATT
D_IN = 3 * D_CONV + 6 * D_ATT
LN_EPS = 1e-5
SM_SCALE = HEAD_DIM ** -0.5
MASK_VALUE = -1e30

COL_Q, COL_K, COL_V, COL_AG, COL_MQ, COL_MG = 6, 7, 8, 9, 10, 11
N_COL_BLOCKS = D_IN // D_ATT
N_GLU_BLOCKS = 4

LANES = 128
SUBLANES = 8
HALO = 32

Q_CHUNKS = 2
Q_TILE = Q_CHUNKS * CHUNK
Q_BAND = ATT_WIN + Q_TILE
N_BIAS_VARIANTS = N_LEFT_CHUNKS // Q_CHUNKS + 1

BF16 = jnp.bfloat16
F32 = jnp.float32

VMEM_LIMIT_BYTES = 56 * 1024 * 1024
ONCE = pl.Buffered(1)


def _compiler_params(n_axes):
    return pltpu.CompilerParams(
        dimension_semantics=("arbitrary",) * n_axes, vmem_limit_bytes=VMEM_LIMIT_BYTES)


def _sigmoid(x):
    return 1.0 / (1.0 + jnp.exp(-x))


def _layer_norm(z, g, b):
    mu = jnp.mean(z, axis=-1, keepdims=True)
    d = z - mu
    var = jnp.mean(d * d, axis=-1, keepdims=True)
    return d * lax.rsqrt(var + LN_EPS) * g + b


IN_PROJ_ROWS = 256


def _store_tail(tail_ref, acc, r0, tm):
    keep = tail_ref.shape[0]
    n = acc.shape[0]
    if r0 + n > tm - keep:
        lo = max(r0, tm - keep)
        tail_ref[lo - (tm - keep):r0 + n - (tm - keep), :] = acc[lo - r0:, :]


def _in_proj_kernel(x_ref, w_ref, h_ref, k_ref, v_ref, xb_ref, tail_ref):
    j = pl.program_id(1)
    tm = x_ref.shape[0]

    @pl.when(j == 0)
    def _():
        xb_ref[...] = x_ref[...].astype(BF16)

    is_sig = (j == 2) | (j == 3)
    is_silu = (j == 4) | (j == 5) | (j == COL_AG) | (j == COL_MG)
    n_rows = min(IN_PROJ_ROWS, tm)
    for r0 in range(0, tm, n_rows):
        rows = slice(r0, r0 + n_rows)
        acc = jnp.dot(xb_ref[rows, :], w_ref[...], preferred_element_type=F32)
        sg = _sigmoid(acc)
        out = jnp.where(is_sig, sg, jnp.where(is_silu, acc * sg, acc))
        h_ref[rows, :] = out.astype(BF16)
        _store_tail(tail_ref, acc, r0, tm)

    @pl.when(j == COL_K)
    def _():
        k_ref[...] = tail_ref[...]

    @pl.when(j == COL_V)
    def _():
        v_ref[...] = tail_ref[...]


def _in_proj(x, w, *, seq_rows, keep):
    m = x.shape[0]
    tm = min(1024, m)
    assert m % tm == 0 and seq_rows % tm == 0 and keep <= tm and tm % min(IN_PROJ_ROWS, tm) == 0
    tiles_per_seq = seq_rows // tm
    n_seq = m // seq_rows
    keep_spec = pl.BlockSpec((keep, D_ATT), lambda i, j: (i // tiles_per_seq, 0))
    return pl.pallas_call(
        _in_proj_kernel,
        grid=(m // tm, N_COL_BLOCKS),
        in_specs=[
            pl.BlockSpec((tm, D_MODEL), lambda i, j: (i, 0)),
            pl.BlockSpec((D_MODEL, D_ATT), lambda i, j: (0, j)),
        ],
        out_specs=[pl.BlockSpec((tm, D_ATT), lambda i, j: (i, j)), keep_spec, keep_spec],
        out_shape=[
            jax.ShapeDtypeStruct((m, D_IN), BF16),
            jax.ShapeDtypeStruct((n_seq * keep, D_ATT), F32),
            jax.ShapeDtypeStruct((n_seq * keep, D_ATT), F32),
        ],
        scratch_shapes=[pltpu.VMEM((tm, D_MODEL), BF16), pltpu.VMEM((keep, D_ATT), F32)],
        compiler_params=_compiler_params(2),
        name="in_proj",
    )(x, w)


TAP_ROWS = 64


def _conv_offsets(s):
    return [o for o in range(HALO - CONV_LEFT, HALO + 1) if o % SUBLANES == s]


def _in_proj_conv_kernel(x_ref, w_ref, ctx_ref, cw_ref, cb_ref, h_ref, y_ref, st_ref, k_ref, v_ref,
                         xb_ref, tail_ref, up_ref, us_ref, *, tiles_per_seq):
    i = pl.program_id(0)
    j = pl.program_id(1)
    tm = x_ref.shape[0]
    n_rows = min(IN_PROJ_ROWS, tm)
    seq_start = i % tiles_per_seq == 0

    @pl.when(j == 0)
    def _():
        xb_ref[...] = x_ref[...].astype(BF16)

    @pl.when((j == 0) & seq_start)
    def _():
        up_ref[0:HALO, :] = ctx_ref[0]

    @pl.when((j == 0) & jnp.logical_not(seq_start))
    def _():
        up_ref[0:HALO, :] = up_ref[tm:tm + HALO, :]

    def matmul_blocks(epilogue):
        for r0 in range(0, tm, n_rows):
            acc = jnp.dot(xb_ref[r0:r0 + n_rows, :], w_ref[...], preferred_element_type=F32)
            epilogue(r0, acc)

    for jj in range(N_GLU_BLOCKS):
        cols = slice((jj % 2) * D_ATT, (jj % 2 + 1) * D_ATT)

        @pl.when(j == jj)
        def _(jj=jj, cols=cols):
            def glu(r0, acc):
                rows = slice(HALO + r0, HALO + r0 + n_rows)
                if jj < 2:
                    up_ref[rows, cols] = acc
                else:
                    up_ref[rows, cols] = up_ref[rows, cols] * _sigmoid(acc)
            matmul_blocks(glu)

    @pl.when(j == N_GLU_BLOCKS)
    def _():
        st_ref[0] = up_ref[HALO + tm - CONV_LEFT:HALO + tm, :]

    @pl.when(j >= N_GLU_BLOCKS)
    def _():
        is_silu = (j == 4) | (j == 5) | (j == COL_AG) | (j == COL_MG)

        lane = pl.ds(pl.multiple_of((j - N_GLU_BLOCKS) * LANES, LANES), LANES)
        groups = TAP_ROWS // SUBLANES

        def shifted_copies():
            n_shifted = tm + HALO - SUBLANES
            us_ref[0, :, :] = up_ref[:, lane]
            for s in range(1, SUBLANES):
                us_ref[s, 0:n_shifted, :] = up_ref[s:s + n_shifted, lane]

        def conv_rows(r0):
            parts = [None, None]
            for s in range(SUBLANES):
                offs = _conv_offsets(s)
                lo = offs[0] - s
                win = us_ref[s, r0 + lo:r0 + offs[-1] - s + TAP_ROWS, :]
                for o in offs:
                    tap = win[o - s - lo:o - s - lo + TAP_ROWS]
                    w = cw_ref[o - (HALO - CONV_LEFT), :, lane]
                    term = tap.reshape(groups, SUBLANES, LANES) * w[None]
                    parts[s % 2] = term if parts[s % 2] is None else parts[s % 2] + term
            acc = (parts[0] + parts[1]).reshape(TAP_ROWS, LANES)
            y_ref[r0:r0 + TAP_ROWS, lane] = acc + cb_ref[0:1, lane]

        n_mm = tm // n_rows
        n_tap_blocks = tm // TAP_ROWS
        weights = [2] + [4] * (n_mm - 2) + [1] if n_mm > 1 else [1]
        bounds = [n_tap_blocks * sum(weights[:r]) // sum(weights) for r in range(n_mm)] + [n_tap_blocks]
        for r in range(n_mm):
            r0 = r * n_rows
            acc = jnp.dot(xb_ref[r0:r0 + n_rows, :], w_ref[...], preferred_element_type=F32)
            if r == 0:
                shifted_copies()
            for blk in range(bounds[r], bounds[r + 1]):
                conv_rows(blk * TAP_ROWS)
            out = jnp.where(is_silu, acc * _sigmoid(acc), acc)
            h_ref[r0:r0 + n_rows, :] = out.astype(BF16)
            _store_tail(tail_ref, acc, r0, tm)

    @pl.when(j == COL_K)
    def _():
        k_ref[...] = tail_ref[...]

    @pl.when(j == COL_V)
    def _():
        v_ref[...] = tail_ref[...]


def _in_proj_conv(x, w, ctx, cw, cb, *, seq_rows, keep):
    m = x.shape[0]
    tm = min(1024, m)
    assert m % tm == 0 and seq_rows % tm == 0 and keep <= tm and tm % min(IN_PROJ_ROWS, tm) == 0
    assert tm % TAP_ROWS == 0 and N_COL_BLOCKS - N_GLU_BLOCKS == D_CONV // LANES
    tiles_per_seq = seq_rows // tm
    n_seq = m // seq_rows
    seq_of = lambda i, j: i // tiles_per_seq
    keep_spec = pl.BlockSpec((keep, D_ATT), lambda i, j: (seq_of(i, j), 0))
    return pl.pallas_call(
        functools.partial(_in_proj_conv_kernel, tiles_per_seq=tiles_per_seq),
        grid=(m // tm, N_COL_BLOCKS),
        in_specs=[
            pl.BlockSpec((tm, D_MODEL), lambda i, j: (i, 0)),
            pl.BlockSpec((D_MODEL, D_ATT), lambda i, j: (0, j)),
            pl.BlockSpec((1, HALO, D_CONV), lambda i, j: (seq_of(i, j), 0, 0)),
            pl.BlockSpec((CONV_W, SUBLANES, D_CONV), lambda i, j: (0, 0, 0), pipeline_mode=ONCE),
            pl.BlockSpec((8, D_CONV), lambda i, j: (0, 0), pipeline_mode=ONCE),
        ],
        out_specs=[
            pl.BlockSpec((tm, D_ATT), lambda i, j: (i, jnp.maximum(j - N_GLU_BLOCKS, 0))),
            pl.BlockSpec((tm, D_CONV), lambda i, j: (i, 0)),
            pl.BlockSpec((1, CONV_LEFT, D_CONV), lambda i, j: (seq_of(i, j), 0, 0)),
            keep_spec, keep_spec,
        ],
        out_shape=[
            jax.ShapeDtypeStruct((m, D_IN - N_GLU_BLOCKS * D_ATT), BF16),
            jax.ShapeDtypeStruct((m, D_CONV), F32),
            jax.ShapeDtypeStruct((n_seq, CONV_LEFT, D_CONV), F32),
            jax.ShapeDtypeStruct((n_seq * keep, D_ATT), F32),
            jax.ShapeDtypeStruct((n_seq * keep, D_ATT), F32),
        ],
        scratch_shapes=[
            pltpu.VMEM((tm, D_MODEL), BF16),
            pltpu.VMEM((keep, D_ATT), F32),
            pltpu.VMEM((HALO + tm, D_CONV), F32),
            pltpu.VMEM((SUBLANES, HALO + tm, LANES), F32),
        ],
        compiler_params=pltpu.CompilerParams(
            dimension_semantics=("arbitrary", "arbitrary"), vmem_limit_bytes=60 * 1024 * 1024),
        name="in_proj_conv",
    )(x, w, ctx, cw, cb)


def _mem_kv_kernel(x_ref, w_ref, k_ref, v_ref, kb_ref, vb_ref):
    acc = jnp.dot(x_ref[...], w_ref[0], preferred_element_type=F32)
    k = acc[:, :D_ATT]
    v = acc[:, D_ATT:]
    k_ref[0] = k
    v_ref[0] = v
    kb_ref[0] = k.astype(BF16)
    vb_ref[0] = v.astype(BF16)


def _mem_kv(mem, w):
    m = mem.shape[0]
    depth = w.shape[0]
    tm = min(1024, m)
    assert m % tm == 0
    out_spec = pl.BlockSpec((1, tm, D_ATT), lambda i, l: (l, i, 0))
    return pl.pallas_call(
        _mem_kv_kernel,
        grid=(m // tm, depth),
        in_specs=[
            pl.BlockSpec((tm, D_MODEL), lambda i, l: (i, 0)),
            pl.BlockSpec((1, D_MODEL, 2 * D_ATT), lambda i, l: (l, 0, 0)),
        ],
        out_specs=[out_spec] * 4,
        out_shape=[jax.ShapeDtypeStruct((depth, m, D_ATT), F32)] * 2
        + [jax.ShapeDtypeStruct((depth, m, D_ATT), BF16)] * 2,
        compiler_params=_compiler_params(2),
        name="mem_kv",
    )(mem, w)


CONV_ROWS = 32
CONV_LANES = 256


def _conv_kernel(a_ref, sb_ref, g_ref, ah_ref, sbh_ref, ctx_ref, cw_ref, cp_ref, wpw_ref,
                 o_ref, st_ref, up_ref, us_ref, y_ref, *, tq, n_t):
    t = pl.program_id(1)

    up_ref[HALO:HALO + tq, :] = a_ref[...].astype(F32) * sb_ref[...].astype(F32)

    @pl.when(t == 0)
    def _():
        up_ref[0:HALO, :] = ctx_ref[0]

    @pl.when(t > 0)
    def _():
        up_ref[0:HALO, :] = ah_ref[...].astype(F32) * sbh_ref[...].astype(F32)

    n_shifted = tq + HALO - SUBLANES
    for s in range(1, SUBLANES):
        us_ref[s - 1, 0:n_shifted, :] = up_ref[s:s + n_shifted, :]

    groups = CONV_ROWS // SUBLANES
    for lc in range(D_CONV // CONV_LANES):
        lanes = slice(lc * CONV_LANES, (lc + 1) * CONV_LANES)

        def row_block(rb, carry, lanes=lanes):
            r0 = pl.multiple_of(rb * CONV_ROWS, CONV_ROWS)
            parts = [None, None]
            for s in range(SUBLANES):
                offs = _conv_offsets(s)
                lo = offs[0] - s
                rows = pl.ds(r0 + lo, offs[-1] - s + CONV_ROWS - lo)
                win = up_ref[rows, lanes] if s == 0 else us_ref[s - 1, rows, lanes]
                for o in offs:
                    tap = win[o - s - lo:o - s - lo + CONV_ROWS]
                    w = cw_ref[o - (HALO - CONV_LEFT), :, lanes]
                    term = tap.reshape(groups, SUBLANES, CONV_LANES) * w[None]
                    parts[s % 2] = term if parts[s % 2] is None else parts[s % 2] + term
            acc = parts[0] + parts[1]
            y_ref[pl.ds(r0, CONV_ROWS), lanes] = acc.reshape(CONV_ROWS, CONV_LANES) + cp_ref[0:1, lanes]
            return carry

        lax.fori_loop(0, tq // CONV_ROWS, row_block, 0)

    yn = _layer_norm(y_ref[...], cp_ref[1:2, :], cp_ref[2:3, :])
    sw = yn * _sigmoid(yn)
    pw = jnp.dot(sw.astype(BF16), wpw_ref[...], preferred_element_type=F32)
    o_ref[...] = (pw * g_ref[...].astype(F32)).astype(BF16)

    @pl.when(t == n_t - 1)
    def _():
        st_ref[0] = up_ref[HALO + tq - CONV_LEFT:HALO + tq, :]


def _conv_branch(h, ctx, cw, cp, wpw, *, n_seq, t_len, tq):
    assert t_len % tq == 0 and tq % HALO == 0 and tq >= HALO
    n_t = t_len // tq
    m = n_seq * t_len

    def row_map(col):
        return lambda b, t: (b * n_t + t, col)

    def halo_map(col):
        return lambda b, t: (jnp.maximum((b * t_len + t * tq) // HALO - 1, 0), col)

    return pl.pallas_call(
        functools.partial(_conv_kernel, tq=tq, n_t=n_t),
        grid=(n_seq, n_t),
        in_specs=[
            pl.BlockSpec((tq, D_CONV), row_map(0)),
            pl.BlockSpec((tq, D_CONV), row_map(1)),
            pl.BlockSpec((tq, D_CONV), row_map(2)),
            pl.BlockSpec((HALO, D_CONV), halo_map(0)),
            pl.BlockSpec((HALO, D_CONV), halo_map(1)),
            pl.BlockSpec((1, HALO, D_CONV), lambda b, t: (b, 0, 0)),
            pl.BlockSpec((CONV_W, SUBLANES, D_CONV), lambda b, t: (0, 0, 0)),
            pl.BlockSpec((8, D_CONV), lambda b, t: (0, 0)),
            pl.BlockSpec((D_CONV, D_CONV), lambda b, t: (0, 0)),
        ],
        out_specs=[
            pl.BlockSpec((tq, D_CONV), lambda b, t: (b * n_t + t, 0)),
            pl.BlockSpec((1, CONV_LEFT, D_CONV), lambda b, t: (b, 0, 0)),
        ],
        out_shape=[
            jax.ShapeDtypeStruct((m, D_CONV), BF16),
            jax.ShapeDtypeStruct((n_seq, CONV_LEFT, D_CONV), F32),
        ],
        scratch_shapes=[
            pltpu.VMEM((HALO + tq, D_CONV), F32),
            pltpu.VMEM((SUBLANES - 1, HALO + tq, D_CONV), F32),
            pltpu.VMEM((tq, D_CONV), F32),
        ],
        compiler_params=_compiler_params(2),
        name="conv_branch",
    )(h, h, h, h, h, ctx, cw, cp, wpw)


def _qk(q, k):
    return lax.dot_general(q, k, (((1,), (1,)), ((), ())), preferred_element_type=F32)


def _attend(qs, ks, vs, biases):
    ss = [_qk(q, k) * SM_SCALE for q, k in zip(qs, ks)]
    ss = [s if b is None else s + b for s, b in zip(ss, biases)]
    ps, ls = [], []
    for s in ss:
        m = jnp.max(s, axis=-1, keepdims=True)
        p = jnp.exp(s - m)
        ls.append(jnp.sum(p, axis=-1, keepdims=True))
        ps.append(p.astype(BF16))
    return [jnp.dot(p, v, preferred_element_type=F32) / l for p, v, l in zip(ps, vs, ls)]


def _head_cols(h):
    return slice(h * HEAD_DIM, (h + 1) * HEAD_DIM)


def _memory_attention(mq_ref, mk_ref, mv_ref, mg_ref, o_ref):
    heads = range(N_HEADS)
    os = _attend([mq_ref[:, _head_cols(h)] for h in heads], [mk_ref[0, :, _head_cols(h)] for h in heads],
                 [mv_ref[0, :, _head_cols(h)] for h in heads], [None] * N_HEADS)
    for h in heads:
        o_ref[:, D_ATT + h * HEAD_DIM:D_ATT + (h + 1) * HEAD_DIM] = (
            os[h] * mg_ref[:, _head_cols(h)].astype(F32)).astype(BF16)


def _attn_prompt_kernel(q_ref, k_ref, v_ref, ag_ref, mq_ref, mg_ref, mk_ref, mv_ref, bias_ref,
                        o_ref, *, tq):
    t = pl.program_id(1)
    n_q = tq // Q_TILE
    heads = range(N_HEADS)

    def q_tile(qi, carry):
        c0 = (t * n_q + qi) * Q_CHUNKS
        start = pl.multiple_of(jnp.maximum(c0 * CHUNK - ATT_WIN, 0), Q_TILE)
        variant = jnp.minimum(c0 // Q_CHUNKS, N_BIAS_VARIANTS - 1)
        rows = pl.ds(pl.multiple_of(qi * Q_TILE, Q_TILE), Q_TILE)
        band = pl.ds(start, Q_BAND)
        os = _attend([q_ref[rows, _head_cols(h)] for h in heads], [k_ref[band, _head_cols(h)] for h in heads],
                     [v_ref[band, _head_cols(h)] for h in heads], [bias_ref[variant, h] for h in heads])
        for h in heads:
            o_ref[rows, _head_cols(h)] = (os[h] * ag_ref[rows, _head_cols(h)].astype(F32)).astype(BF16)
        return carry

    lax.fori_loop(0, n_q, q_tile, 0)
    _memory_attention(mq_ref, mk_ref, mv_ref, mg_ref, o_ref)


def _attn_prompt(h, mk, mv, bias, *, n_seq, t_len, tq, col0):
    assert t_len % tq == 0 and tq % Q_TILE == 0 and t_len >= Q_BAND
    n_t = t_len // tq

    def row_map(col):
        return lambda b, t: (b * n_t + t, col0 + col)

    return pl.pallas_call(
        functools.partial(_attn_prompt_kernel, tq=tq),
        grid=(n_seq, n_t),
        in_specs=[
            pl.BlockSpec((tq, D_ATT), row_map(COL_Q)),
            pl.BlockSpec((t_len, D_ATT), lambda b, t: (b, col0 + COL_K)),
            pl.BlockSpec((t_len, D_ATT), lambda b, t: (b, col0 + COL_V)),
            pl.BlockSpec((tq, D_ATT), row_map(COL_AG)),
            pl.BlockSpec((tq, D_ATT), row_map(COL_MQ)),
            pl.BlockSpec((tq, D_ATT), row_map(COL_MG)),
            pl.BlockSpec((1, N_MEM, D_ATT), lambda b, t: (b, 0, 0)),
            pl.BlockSpec((1, N_MEM, D_ATT), lambda b, t: (b, 0, 0)),
            pl.BlockSpec(bias.shape, lambda b, t: (0, 0, 0, 0)),
        ],
        out_specs=pl.BlockSpec((tq, 2 * D_ATT), lambda b, t: (b * n_t + t, 0)),
        out_shape=jax.ShapeDtypeStruct((n_seq * t_len, 2 * D_ATT), BF16),
        compiler_params=_compiler_params(2),
        name="attn_prompt",
    )(h, h, h, h, h, h, mk, mv, bias)


def _attn_sample_kernel(q_ref, k_ref, v_ref, ag_ref, mq_ref, mg_ref, kc_ref, vc_ref, mk_ref, mv_ref,
                        bias_ref, o_ref):
    heads = range(N_HEADS)
    ks = [jnp.concatenate([kc_ref[0, :, _head_cols(h)], k_ref[:, _head_cols(h)]], axis=0) for h in heads]
    vs = [jnp.concatenate([vc_ref[0, :, _head_cols(h)], v_ref[:, _head_cols(h)]], axis=0) for h in heads]
    os = _attend([q_ref[:, _head_cols(h)] for h in heads], ks, vs, [bias_ref[h] for h in heads])
    for h in heads:
        o_ref[:, _head_cols(h)] = (os[h] * ag_ref[:, _head_cols(h)].astype(F32)).astype(BF16)
    _memory_attention(mq_ref, mk_ref, mv_ref, mg_ref, o_ref)


def _attn_sample(h, kc, vc, mk, mv, bias, *, n_seq):
    def row_map(col):
        return lambda b: (b, col)

    seq_spec = lambda n: pl.BlockSpec((1, n, D_ATT), lambda b: (b, 0, 0))
    return pl.pallas_call(
        _attn_sample_kernel,
        grid=(n_seq,),
        in_specs=[
            pl.BlockSpec((CHUNK, D_ATT), row_map(COL_Q)),
            pl.BlockSpec((CHUNK, D_ATT), row_map(COL_K)),
            pl.BlockSpec((CHUNK, D_ATT), row_map(COL_V)),
            pl.BlockSpec((CHUNK, D_ATT), row_map(COL_AG)),
            pl.BlockSpec((CHUNK, D_ATT), row_map(COL_MQ)),
            pl.BlockSpec((CHUNK, D_ATT), row_map(COL_MG)),
            seq_spec(ATT_WIN), seq_spec(ATT_WIN), seq_spec(N_MEM), seq_spec(N_MEM),
            pl.BlockSpec(bias.shape, lambda b: (0, 0, 0)),
        ],
        out_specs=pl.BlockSpec((CHUNK, 2 * D_ATT), lambda b: (b, 0)),
        out_shape=jax.ShapeDtypeStruct((n_seq * CHUNK, 2 * D_ATT), BF16),
        compiler_params=_compiler_params(1),
        name="attn_sample",
    )(h, h, h, h, h, h, kc, vc, mk, mv, bias)


OUT_COLS = 512


def _out_proj_kernel(cm_ref, am_ref, x_ref, w_ref, gb_ref, y_ref, z_ref, *, alpha):
    tm = x_ref.shape[0]
    row_sum = jnp.zeros((tm, 1), F32)
    for n in range(D_MODEL // OUT_COLS):
        cols = slice(n * OUT_COLS, (n + 1) * OUT_COLS)
        z = jnp.dot(cm_ref[...], w_ref[0:D_CONV, cols], preferred_element_type=F32)
        z = z + jnp.dot(am_ref[...], w_ref[D_CONV:D_MIX, cols], preferred_element_type=F32)
        z = alpha * x_ref[:, cols] + z
        z_ref[:, cols] = z
        row_sum = row_sum + jnp.sum(z, axis=-1, keepdims=True)
    mu = row_sum * (1.0 / D_MODEL)
    sq_sum = jnp.zeros((tm, 1), F32)
    for n in range(D_MODEL // OUT_COLS):
        cols = slice(n * OUT_COLS, (n + 1) * OUT_COLS)
        d = z_ref[:, cols] - mu
        sq_sum = sq_sum + jnp.sum(d * d, axis=-1, keepdims=True)
    inv = lax.rsqrt(sq_sum * (1.0 / D_MODEL) + LN_EPS)
    for n in range(D_MODEL // OUT_COLS):
        cols = slice(n * OUT_COLS, (n + 1) * OUT_COLS)
        y_ref[:, cols] = (z_ref[:, cols] - mu) * inv * gb_ref[0:1, cols] + gb_ref[1:2, cols]


def _out_proj(cm, am, x, w, gb, *, alpha):
    m = x.shape[0]
    tm = min(512, m)
    assert m % tm == 0
    return pl.pallas_call(
        functools.partial(_out_proj_kernel, alpha=alpha),
        grid=(m // tm,),
        in_specs=[
            pl.BlockSpec((tm, D_CONV), lambda i: (i, 0)),
            pl.BlockSpec((tm, 2 * D_ATT), lambda i: (i, 0)),
            pl.BlockSpec((tm, D_MODEL), lambda i: (i, 0)),
            pl.BlockSpec((D_MIX, D_MODEL), lambda i: (0, 0)),
            pl.BlockSpec((8, D_MODEL), lambda i: (0, 0)),
        ],
        out_specs=pl.BlockSpec((tm, D_MODEL), lambda i: (i, 0)),
        out_shape=jax.ShapeDtypeStruct((m, D_MODEL), F32),
        scratch_shapes=[pltpu.VMEM((tm, D_MODEL), F32)],
        compiler_params=_compiler_params(1),
        name="out_proj",
    )(cm, am, x, w, gb)


OUT_ROWS = 256


def _out_proj_conv_kernel(yc_ref, g_ref, am_ref, x_ref, cp_ref, wpw_ref, w_ref, gb_ref, y_ref, *, alpha):
    tm = x_ref.shape[0]
    n_rows = min(OUT_ROWS, tm)
    for r0 in range(0, tm, n_rows):
        rows = slice(r0, r0 + n_rows)
        yn = _layer_norm(yc_ref[rows, :], cp_ref[1:2, :], cp_ref[2:3, :])
        sw = yn * _sigmoid(yn)
        pw = jnp.dot(sw.astype(BF16), wpw_ref[...], preferred_element_type=F32)
        cm = (pw * g_ref[rows, :].astype(F32)).astype(BF16)
        z = jnp.dot(cm, w_ref[0:D_CONV, :], preferred_element_type=F32)
        z = z + jnp.dot(am_ref[rows, :], w_ref[D_CONV:D_MIX, :], preferred_element_type=F32)
        z = z + alpha * x_ref[rows, :]
        y_ref[rows, :] = _layer_norm(z, gb_ref[0:1, :], gb_ref[1:2, :])


def _out_proj_conv(yc, h, am, x, cp, wpw, w, gb, *, alpha):
    m = x.shape[0]
    tm = min(512, m)
    assert m % tm == 0 and tm % min(OUT_ROWS, tm) == 0
    rows_of = lambda width: pl.BlockSpec((tm, width), lambda i: (i, 0))
    whole = lambda shape: pl.BlockSpec(shape, lambda i: (0, 0), pipeline_mode=ONCE)
    return pl.pallas_call(
        functools.partial(_out_proj_conv_kernel, alpha=alpha),
        grid=(m // tm,),
        in_specs=[
            rows_of(D_CONV), rows_of(D_CONV), rows_of(2 * D_ATT), rows_of(D_MODEL),
            whole((8, D_CONV)), whole((D_CONV, D_CONV)), whole((D_MIX, D_MODEL)), whole((8, D_MODEL)),
        ],
        out_specs=rows_of(D_MODEL),
        out_shape=jax.ShapeDtypeStruct((m, D_MODEL), F32),
        compiler_params=_compiler_params(1),
        name="out_proj_conv",
    )(yc, h, am, x, cp, wpw, w, gb)


def _band_bias(table):
    n_h = table.shape[0]
    n_u = Q_TILE + Q_BAND - 1
    n_flat = ATT_WIN + Q_TILE - REL_CLIP
    lo = REL_CLIP - (Q_TILE - 1)
    assert lo >= 0 and n_flat + 2 * REL_CLIP - lo == n_u
    u = jnp.concatenate([jnp.broadcast_to(table[:, 2 * REL_CLIP:], (n_h, n_flat)),
                         jnp.flip(table[:, lo:2 * REL_CLIP], axis=1)], axis=1).astype(F32)
    u = jnp.pad(u, ((0, 0), (0, 1)))
    skew = jnp.broadcast_to(u[:, None, :], (n_h, Q_TILE, n_u + 1)).reshape(n_h, Q_TILE * (n_u + 1))
    full = skew[:, Q_TILE - 1:Q_TILE - 1 + Q_TILE * n_u].reshape(n_h, Q_TILE, n_u)[:, :, :Q_BAND]
    qc = jnp.arange(Q_TILE)[:, None] // CHUNK
    kc = jnp.arange(Q_BAND)[None, :] // CHUNK
    full = jnp.where((kc >= qc) & (kc <= qc + N_LEFT_CHUNKS), full, MASK_VALUE)
    variants = []
    for v in range(N_BIAS_VARIANTS - 1):
        off = ATT_WIN - v * Q_TILE
        pad = jnp.full((n_h, Q_TILE, off), MASK_VALUE, F32)
        variants.append(jnp.concatenate([full[:, :, off:], pad], axis=-1))
    variants.append(full)
    return jnp.stack(variants)


def _pack_rows(rows, width):
    return jnp.concatenate([jnp.stack(rows).astype(F32), jnp.zeros((8 - len(rows), width), F32)])


def kernel(x_prompt, x_sample, mem_prompt, cache_conv, cache_att_k, cache_att_v, cache_mem_k, cache_mem_v,
           w_in, conv_w, conv_b, conv_ln_g, conv_ln_b, w_pw, rel_table, w_mem_kv, w_out, ln_g, ln_b):
    n_p, t_p, _ = x_prompt.shape
    n_s, t_s, _ = x_sample.shape
    depth = w_in.shape[0]
    n_mem = mem_prompt.shape[1]
    assert t_s == CHUNK and n_mem == N_MEM and cache_att_k.shape[2] == ATT_WIN
    alpha = (2 * depth) ** 0.25
    keep = min(ATT_WIN, t_p)

    w_in_b = w_in.astype(BF16)
    w_pw_b = w_pw.astype(BF16)
    w_out_b = w_out.astype(BF16)
    w_mkv_b = w_mem_kv.astype(BF16)

    mk_p, mv_p, mk_pb, mv_pb = _mem_kv(mem_prompt.reshape(n_p * n_mem, D_MODEL).astype(BF16), w_mkv_b)
    mk_pb = mk_pb.reshape(depth, n_p, n_mem, D_ATT)
    mv_pb = mv_pb.reshape(depth, n_p, n_mem, D_ATT)

    xp = x_prompt.reshape(n_p * t_p, D_MODEL)
    xs = x_sample.reshape(n_s * t_s, D_MODEL)
    zero_ctx = jnp.zeros((n_p, HALO, D_CONV), F32)
    cache_ctx = jnp.pad(cache_conv, ((0, 0), (0, 0), (HALO - CONV_LEFT, 0), (0, 0)))
    kc_b = cache_att_k.reshape(depth, n_s, ATT_WIN, D_ATT).astype(BF16)
    vc_b = cache_att_v.reshape(depth, n_s, ATT_WIN, D_ATT).astype(BF16)
    mk_sb = cache_mem_k.reshape(depth, n_s, n_mem, D_ATT).astype(BF16)
    mv_sb = cache_mem_v.reshape(depth, n_s, n_mem, D_ATT).astype(BF16)

    conv_p, kp_l, vp_l, conv_s, ks_l, vs_l = [], [], [], [], [], []
    for l in range(depth):
        cw = jnp.broadcast_to(conv_w[l][:, None, :], (CONV_W, SUBLANES, D_CONV))
        cp = _pack_rows([conv_b[l], conv_ln_g[l], conv_ln_b[l]], D_CONV)
        gb = _pack_rows([ln_g[l], ln_b[l]], D_MODEL)
        bias = _band_bias(rel_table[l])

        h, yc, cs, k_new, v_new = _in_proj_conv(xp, w_in_b[l], zero_ctx, cw, cp, seq_rows=t_p, keep=keep)
        am = _attn_prompt(h, mk_pb[l], mv_pb[l], bias, n_seq=n_p, t_len=t_p, tq=512, col0=-N_GLU_BLOCKS)
        xp = _out_proj_conv(yc, h, am, xp, cp, w_pw_b[l], w_out_b[l], gb, alpha=alpha)
        conv_p.append(cs)
        kp_l.append(k_new)
        vp_l.append(v_new)

        h, k_new, v_new = _in_proj(xs, w_in_b[l], seq_rows=n_s * t_s, keep=n_s * t_s)
        cm, cs = _conv_branch(h, cache_ctx[l], cw, cp, w_pw_b[l], n_seq=n_s, t_len=t_s, tq=t_s)
        am = _attn_sample(h, kc_b[l], vc_b[l], mk_sb[l], mv_sb[l], bias[-1, :, :CHUNK, :BAND], n_seq=n_s)
        xs = _out_proj(cm, am, xs, w_out_b[l], gb, alpha=alpha)
        conv_s.append(cs)
        ks_l.append(k_new)
        vs_l.append(v_new)

    mem_shape = (depth, n_p, n_mem, N_HEADS, HEAD_DIM)
    kv_p_shape = (depth, n_p, keep, N_HEADS, HEAD_DIM)
    kv_s_shape = (depth, n_s, t_s, N_HEADS, HEAD_DIM)
    return (xp.reshape(n_p, t_p, D_MODEL), xs.reshape(n_s, t_s, D_MODEL), jnp.stack(conv_p),
            jnp.stack(kp_l).reshape(kv_p_shape), jnp.stack(vp_l).reshape(kv_p_shape),
            mk_p.reshape(mem_shape), mv_p.reshape(mem_shape),
            jnp.stack(conv_s), jnp.stack(ks_l).reshape(kv_s_shape), jnp.stack(vs_l).reshape(kv_s_shape))
```

```python
import functools

import jax
import jax.numpy as jnp
from jax import lax
from jax.experimental import pallas as pl
from jax.experimental.pallas import tpu as pltpu

D_MODEL = 2048
CHUNK = 64
N_LEFT_CHUNKS = 8
ATT_WIN = N_LEFT_CHUNKS * CHUNK
BAND = ATT_WIN + CHUNK
CONV_W = 31
CONV_LEFT = CONV_W - 1
D_CONV = D_MODEL // 2
HEAD_DIM = 128
N_HEADS = 4
D_ATT = N_HEADS * HEAD_DIM
N_MEM = 256
REL_CLIP = 128
D_MIX = D_CONV + 2 * D_ATT
D_IN = 3 * D_CONV + 6 * D_ATT
LN_EPS = 1e-5
SM_SCALE = HEAD_DIM ** -0.5
MASK_VALUE = -1e30

COL_Q, COL_K, COL_V, COL_AG, COL_MQ, COL_MG = 6, 7, 8, 9, 10, 11
N_COL_BLOCKS = D_IN // D_ATT
N_GLU_BLOCKS = 4

LANES = 128
SUBLANES = 8
HALO = 32

Q_CHUNKS = 2
Q_TILE = Q_CHUNKS * CHUNK
Q_BAND = ATT_WIN + Q_TILE
N_BIAS_VARIANTS = N_LEFT_CHUNKS // Q_CHUNKS + 1

BF16 = jnp.bfloat16
F32 = jnp.float32

VMEM_LIMIT_BYTES = 56 * 1024 * 1024
ONCE = pl.Buffered(1)


def _compiler_params(n_axes):
    return pltpu.CompilerParams(
        dimension_semantics=("arbitrary",) * n_axes, vmem_limit_bytes=VMEM_LIMIT_BYTES)


def _sigmoid(x):
    return 1.0 / (1.0 + jnp.exp(-x))


def _layer_norm(z, g, b):
    mu = jnp.mean(z, axis=-1, keepdims=True)
    d = z - mu
    var = jnp.mean(d * d, axis=-1, keepdims=True)
    return d * lax.rsqrt(var + LN_EPS) * g + b


IN_PROJ_ROWS = 256


def _store_tail(tail_ref, acc, r0, tm):
    keep = tail_ref.shape[0]
    n = acc.shape[0]
    if r0 + n > tm - keep:
        lo = max(r0, tm - keep)
        tail_ref[lo - (tm - keep):r0 + n - (tm - keep), :] = acc[lo - r0:, :]


def _in_proj_kernel(x_ref, w_ref, h_ref, k_ref, v_ref, xb_ref, tail_ref):
    j = pl.program_id(1)
    tm = x_ref.shape[0]

    @pl.when(j == 0)
    def _():
        xb_ref[...] = x_ref[...].astype(BF16)

    is_sig = (j == 2) | (j == 3)
    is_silu = (j == 4) | (j == 5) | (j == COL_AG) | (j == COL_MG)
    n_rows = min(IN_PROJ_ROWS, tm)
    for r0 in range(0, tm, n_rows):
        rows = slice(r0, r0 + n_rows)
        acc = jnp.dot(xb_ref[rows, :], w_ref[0], preferred_element_type=F32)
        sg = _sigmoid(acc)
        out = jnp.where(is_sig, sg, jnp.where(is_silu, acc * sg, acc))
        h_ref[rows, :] = out.astype(BF16)
        _store_tail(tail_ref, acc, r0, tm)

    @pl.when(j == COL_K)
    def _():
        k_ref[...] = tail_ref[...]

    @pl.when(j == COL_V)
    def _():
        v_ref[...] = tail_ref[...]


def _in_proj(x, w, layer, *, seq_rows, keep):
    m = x.shape[0]
    tm = min(1024, m)
    assert m % tm == 0 and seq_rows % tm == 0 and keep <= tm and tm % min(IN_PROJ_ROWS, tm) == 0
    tiles_per_seq = seq_rows // tm
    n_seq = m // seq_rows
    keep_spec = pl.BlockSpec((keep, D_ATT), lambda i, j: (i // tiles_per_seq, 0))
    return pl.pallas_call(
        _in_proj_kernel,
        grid=(m // tm, N_COL_BLOCKS),
        in_specs=[
            pl.BlockSpec((tm, D_MODEL), lambda i, j: (i, 0)),
            pl.BlockSpec((1, D_MODEL, D_ATT), lambda i, j: (layer, 0, j)),
        ],
        out_specs=[pl.BlockSpec((tm, D_ATT), lambda i, j: (i, j)), keep_spec, keep_spec],
        out_shape=[
            jax.ShapeDtypeStruct((m, D_IN), BF16),
            jax.ShapeDtypeStruct((n_seq * keep, D_ATT), F32),
            jax.ShapeDtypeStruct((n_seq * keep, D_ATT), F32),
        ],
        scratch_shapes=[pltpu.VMEM((tm, D_MODEL), BF16), pltpu.VMEM((keep, D_ATT), F32)],
        compiler_params=_compiler_params(2),
        name="in_proj",
    )(x, w)


TAP_ROWS = 32


def _conv_offsets(s):
    return [o for o in range(HALO - CONV_LEFT, HALO + 1) if o % SUBLANES == s]


def _in_proj_conv_kernel(x_ref, w_ref, ctx_ref, cw_ref, cb_ref, h_ref, y_ref, st_ref, k_ref, v_ref,
                         xb_ref, tail_ref, up_ref, *, tiles_per_seq):
    i = pl.program_id(0)
    j = pl.program_id(1)
    tm = x_ref.shape[0]
    n_rows = min(IN_PROJ_ROWS, tm)
    seq_start = i % tiles_per_seq == 0

    @pl.when(j == 0)
    def _():
        xb_ref[...] = x_ref[...].astype(BF16)

    @pl.when((j == 0) & seq_start)
    def _():
        up_ref[0:HALO, :] = ctx_ref[0]

    @pl.when((j == 0) & jnp.logical_not(seq_start))
    def _():
        up_ref[0:HALO, :] = up_ref[tm:tm + HALO, :]

    def matmul_blocks(epilogue):
        for r0 in range(0, tm, n_rows):
            acc = jnp.dot(xb_ref[r0:r0 + n_rows, :], w_ref[0], preferred_element_type=F32)
            epilogue(r0, acc)

    for jj in range(N_GLU_BLOCKS):
        cols = slice((jj % 2) * D_ATT, (jj % 2 + 1) * D_ATT)

        @pl.when(j == jj)
        def _(jj=jj, cols=cols):
            def glu(r0, acc):
                rows = slice(HALO + r0, HALO + r0 + n_rows)
                if jj < 2:
                    up_ref[rows, cols] = acc
                else:
                    up_ref[rows, cols] = up_ref[rows, cols] * _sigmoid(acc)
            matmul_blocks(glu)

    @pl.when(j == N_GLU_BLOCKS)
    def _():
        st_ref[0] = up_ref[HALO + tm - CONV_LEFT:HALO + tm, :]

    @pl.when(j >= N_GLU_BLOCKS)
    def _():
        is_silu = (j == 4) | (j == 5) | (j == COL_AG) | (j == COL_MG)

        lane = pl.ds(pl.multiple_of((j - N_GLU_BLOCKS) * LANES, LANES), LANES)
        groups = TAP_ROWS // SUBLANES

        def conv_rows(r0):
            base = up_ref[r0:r0 + HALO + TAP_ROWS, lane]
            parts = [None, None]
            for s in range(SUBLANES):
                offs = _conv_offsets(s)
                lo = offs[0] - s
                win = base[s + lo:offs[-1] + TAP_ROWS]
                for o in offs:
                    tap = win[o - s - lo:o - s - lo + TAP_ROWS]
                    w = cw_ref[o - (HALO - CONV_LEFT), :, lane]
                    term = tap.reshape(groups, SUBLANES, LANES) * w[None]
                    parts[s % 2] = term if parts[s % 2] is None else parts[s % 2] + term
            acc = (parts[0] + parts[1]).reshape(TAP_ROWS, LANES)
            y_ref[r0:r0 + TAP_ROWS, lane] = acc + cb_ref[0:1, lane]

        n_mm = tm // n_rows
        n_tap_blocks = tm // TAP_ROWS
        weights = [3] * (n_mm - 1) + [1]
        bounds = [n_tap_blocks * sum(weights[:r]) // sum(weights) for r in range(n_mm)] + [n_tap_blocks]
        for r in range(n_mm):
            r0 = r * n_rows
            acc = jnp.dot(xb_ref[r0:r0 + n_rows, :], w_ref[0], preferred_element_type=F32)
            for blk in range(bounds[r], bounds[r + 1]):
                conv_rows(blk * TAP_ROWS)
            out = jnp.where(is_silu, acc * _sigmoid(acc), acc)
            h_ref[r0:r0 + n_rows, :] = out.astype(BF16)
            _store_tail(tail_ref, acc, r0, tm)

    @pl.when(j == COL_K)
    def _():
        k_ref[...] = tail_ref[...]

    @pl.when(j == COL_V)
    def _():
        v_ref[...] = tail_ref[...]


def _in_proj_conv(x, w, layer, ctx, cw, cb, *, seq_rows, keep):
    m = x.shape[0]
    tm = min(1024, m)
    assert m % tm == 0 and seq_rows % tm == 0 and keep <= tm and tm % min(IN_PROJ_ROWS, tm) == 0
    assert tm % TAP_ROWS == 0 and N_COL_BLOCKS - N_GLU_BLOCKS == D_CONV // LANES
    tiles_per_seq = seq_rows // tm
    n_seq = m // seq_rows
    seq_of = lambda i, j: i // tiles_per_seq
    keep_spec = pl.BlockSpec((keep, D_ATT), lambda i, j: (seq_of(i, j), 0))
    return pl.pallas_call(
        functools.partial(_in_proj_conv_kernel, tiles_per_seq=tiles_per_seq),
        grid=(m // tm, N_COL_BLOCKS),
        in_specs=[
            pl.BlockSpec((tm, D_MODEL), lambda i, j: (i, 0)),
            pl.BlockSpec((1, D_MODEL, D_ATT), lambda i, j: (layer, 0, j)),
            pl.BlockSpec((1, HALO, D_CONV), lambda i, j: (seq_of(i, j), 0, 0)),
            pl.BlockSpec((CONV_W, SUBLANES, D_CONV), lambda i, j: (0, 0, 0), pipeline_mode=ONCE),
            pl.BlockSpec((8, D_CONV), lambda i, j: (0, 0), pipeline_mode=ONCE),
        ],
        out_specs=[
            pl.BlockSpec((tm, D_ATT), lambda i, j: (i, jnp.maximum(j - N_GLU_BLOCKS, 0))),
            pl.BlockSpec((tm, D_CONV), lambda i, j: (i, 0)),
            pl.BlockSpec((1, CONV_LEFT, D_CONV), lambda i, j: (seq_of(i, j), 0, 0)),
            keep_spec, keep_spec,
        ],
        out_shape=[
            jax.ShapeDtypeStruct((m, D_IN - N_GLU_BLOCKS * D_ATT), BF16),
            jax.ShapeDtypeStruct((m, D_CONV), F32),
            jax.ShapeDtypeStruct((n_seq, CONV_LEFT, D_CONV), F32),
            jax.ShapeDtypeStruct((n_seq * keep, D_ATT), F32),
            jax.ShapeDtypeStruct((n_seq * keep, D_ATT), F32),
        ],
        scratch_shapes=[
            pltpu.VMEM((tm, D_MODEL), BF16),
            pltpu.VMEM((keep, D_ATT), F32),
            pltpu.VMEM((HALO + tm, D_CONV), F32),
        ],
        compiler_params=pltpu.CompilerParams(
            dimension_semantics=("arbitrary", "arbitrary"), vmem_limit_bytes=60 * 1024 * 1024),
        name="in_proj_conv",
    )(x, w, ctx, cw, cb)


def _mem_kv_kernel(x_ref, w_ref, k_ref, v_ref, kb_ref, vb_ref):
    acc = jnp.dot(x_ref[...], w_ref[0], preferred_element_type=F32)
    k = acc[:, :D_ATT]
    v = acc[:, D_ATT:]
    k_ref[0] = k
    v_ref[0] = v
    kb_ref[0] = k.astype(BF16)
    vb_ref[0] = v.astype(BF16)


def _mem_kv(mem, w):
    m = mem.shape[0]
    depth = w.shape[0]
    tm = min(1024, m)
    assert m % tm == 0
    out_spec = pl.BlockSpec((1, tm, D_ATT), lambda i, l: (l, i, 0))
    return pl.pallas_call(
        _mem_kv_kernel,
        grid=(m // tm, depth),
        in_specs=[
            pl.BlockSpec((tm, D_MODEL), lambda i, l: (i, 0)),
            pl.BlockSpec((1, D_MODEL, 2 * D_ATT), lambda i, l: (l, 0, 0)),
        ],
        out_specs=[out_spec] * 4,
        out_shape=[jax.ShapeDtypeStruct((depth, m, D_ATT), F32)] * 2
        + [jax.ShapeDtypeStruct((depth, m, D_ATT), BF16)] * 2,
        compiler_params=_compiler_params(2),
        name="mem_kv",
    )(mem, w)


CONV_ROWS = 32
CONV_LANES = 256


def _conv_kernel(a_ref, sb_ref, g_ref, ah_ref, sbh_ref, ctx_ref, cw_ref, cp_ref, wpw_ref,
                 o_ref, st_ref, up_ref, us_ref, y_ref, *, tq, n_t):
    t = pl.program_id(1)

    up_ref[HALO:HALO + tq, :] = a_ref[...].astype(F32) * sb_ref[...].astype(F32)

    @pl.when(t == 0)
    def _():
        up_ref[0:HALO, :] = ctx_ref[0]

    @pl.when(t > 0)
    def _():
        up_ref[0:HALO, :] = ah_ref[...].astype(F32) * sbh_ref[...].astype(F32)

    n_shifted = tq + HALO - SUBLANES
    for s in range(1, SUBLANES):
        us_ref[s - 1, 0:n_shifted, :] = up_ref[s:s + n_shifted, :]

    groups = CONV_ROWS // SUBLANES
    for lc in range(D_CONV // CONV_LANES):
        lanes = slice(lc * CONV_LANES, (lc + 1) * CONV_LANES)

        def row_block(rb, carry, lanes=lanes):
            r0 = pl.multiple_of(rb * CONV_ROWS, CONV_ROWS)
            parts = [None, None]
            for s in range(SUBLANES):
                offs = _conv_offsets(s)
                lo = offs[0] - s
                rows = pl.ds(r0 + lo, offs[-1] - s + CONV_ROWS - lo)
                win = up_ref[rows, lanes] if s == 0 else us_ref[s - 1, rows, lanes]
                for o in offs:
                    tap = win[o - s - lo:o - s - lo + CONV_ROWS]
                    w = cw_ref[o - (HALO - CONV_LEFT), :, lanes]
                    term = tap.reshape(groups, SUBLANES, CONV_LANES) * w[None]
                    parts[s % 2] = term if parts[s % 2] is None else parts[s % 2] + term
            acc = parts[0] + parts[1]
            y_ref[pl.ds(r0, CONV_ROWS), lanes] = acc.reshape(CONV_ROWS, CONV_LANES) + cp_ref[0:1, lanes]
            return carry

        lax.fori_loop(0, tq // CONV_ROWS, row_block, 0)

    yn = _layer_norm(y_ref[...], cp_ref[1:2, :], cp_ref[2:3, :])
    sw = yn * _sigmoid(yn)
    pw = jnp.dot(sw.astype(BF16), wpw_ref[0], preferred_element_type=F32)
    o_ref[...] = (pw * g_ref[...].astype(F32)).astype(BF16)

    @pl.when(t == n_t - 1)
    def _():
        st_ref[0] = up_ref[HALO + tq - CONV_LEFT:HALO + tq, :]


def _conv_branch(h, ctx, cw, cp, wpw, layer, *, n_seq, t_len, tq):
    assert t_len % tq == 0 and tq % HALO == 0 and tq >= HALO
    n_t = t_len // tq
    m = n_seq * t_len

    def row_map(col):
        return lambda b, t: (b * n_t + t, col)

    def halo_map(col):
        return lambda b, t: (jnp.maximum((b * t_len + t * tq) // HALO - 1, 0), col)

    return pl.pallas_call(
        functools.partial(_conv_kernel, tq=tq, n_t=n_t),
        grid=(n_seq, n_t),
        in_specs=[
            pl.BlockSpec((tq, D_CONV), row_map(0)),
            pl.BlockSpec((tq, D_CONV), row_map(1)),
            pl.BlockSpec((tq, D_CONV), row_map(2)),
            pl.BlockSpec((HALO, D_CONV), halo_map(0)),
            pl.BlockSpec((HALO, D_CONV), halo_map(1)),
            pl.BlockSpec((1, HALO, D_CONV), lambda b, t: (b, 0, 0)),
            pl.BlockSpec((CONV_W, SUBLANES, D_CONV), lambda b, t: (0, 0, 0)),
            pl.BlockSpec((8, D_CONV), lambda b, t: (0, 0)),
            pl.BlockSpec((1, D_CONV, D_CONV), lambda b, t: (layer, 0, 0)),
        ],
        out_specs=[
            pl.BlockSpec((tq, D_CONV), lambda b, t: (b * n_t + t, 0)),
            pl.BlockSpec((1, CONV_LEFT, D_CONV), lambda b, t: (b, 0, 0)),
        ],
        out_shape=[
            jax.ShapeDtypeStruct((m, D_CONV), BF16),
            jax.ShapeDtypeStruct((n_seq, CONV_LEFT, D_CONV), F32),
        ],
        scratch_shapes=[
            pltpu.VMEM((HALO + tq, D_CONV), F32),
            pltpu.VMEM((SUBLANES - 1, HALO + tq, D_CONV), F32),
            pltpu.VMEM((tq, D_CONV), F32),
        ],
        compiler_params=_compiler_params(2),
        name="conv_branch",
    )(h, h, h, h, h, ctx, cw, cp, wpw)


def _qk(q, k):
    return lax.dot_general(q, k, (((1,), (1,)), ((), ())), preferred_element_type=F32)


def _attend(qs, ks, vs, biases):
    ss = [_qk(q, k) * SM_SCALE for q, k in zip(qs, ks)]
    ss = [s if b is None else s + b for s, b in zip(ss, biases)]
    ps, ls = [], []
    for s in ss:
        m = jnp.max(s, axis=-1, keepdims=True)
        p = jnp.exp(s - m)
        ls.append(jnp.sum(p, axis=-1, keepdims=True))
        ps.append(p.astype(BF16))
    return [jnp.dot(p, v, preferred_element_type=F32) / l for p, v, l in zip(ps, vs, ls)]


def _head_cols(h):
    return slice(h * HEAD_DIM, (h + 1) * HEAD_DIM)


def _memory_attention(mq_ref, mk_ref, mv_ref, mg_ref, o_ref):
    heads = range(N_HEADS)
    os = _attend([mq_ref[:, _head_cols(h)] for h in heads], [mk_ref[0, :, _head_cols(h)] for h in heads],
                 [mv_ref[0, :, _head_cols(h)] for h in heads], [None] * N_HEADS)
    for h in heads:
        o_ref[:, D_ATT + h * HEAD_DIM:D_ATT + (h + 1) * HEAD_DIM] = (
            os[h] * mg_ref[:, _head_cols(h)].astype(F32)).astype(BF16)


def _attn_prompt_kernel(q_ref, k_ref, v_ref, ag_ref, mq_ref, mg_ref, mk_ref, mv_ref, bias_ref,
                        o_ref, *, tq):
    t = pl.program_id(1)
    n_q = tq // Q_TILE
    heads = range(N_HEADS)

    def q_tile(qi, carry):
        c0 = (t * n_q + qi) * Q_CHUNKS
        start = pl.multiple_of(jnp.maximum(c0 * CHUNK - ATT_WIN, 0), Q_TILE)
        variant = jnp.minimum(c0 // Q_CHUNKS, N_BIAS_VARIANTS - 1)
        rows = pl.ds(pl.multiple_of(qi * Q_TILE, Q_TILE), Q_TILE)
        band = pl.ds(start, Q_BAND)
        os = _attend([q_ref[rows, _head_cols(h)] for h in heads], [k_ref[band, _head_cols(h)] for h in heads],
                     [v_ref[band, _head_cols(h)] for h in heads], [bias_ref[variant, h] for h in heads])
        for h in heads:
            o_ref[rows, _head_cols(h)] = (os[h] * ag_ref[rows, _head_cols(h)].astype(F32)).astype(BF16)
        return carry

    lax.fori_loop(0, n_q, q_tile, 0)
    _memory_attention(mq_ref, mk_ref, mv_ref, mg_ref, o_ref)


def _attn_prompt(h, mk, mv, bias, *, n_seq, t_len, tq, col0):
    assert t_len % tq == 0 and tq % Q_TILE == 0 and t_len >= Q_BAND
    n_t = t_len // tq

    def row_map(col):
        return lambda b, t: (b * n_t + t, col0 + col)

    return pl.pallas_call(
        functools.partial(_attn_prompt_kernel, tq=tq),
        grid=(n_seq, n_t),
        in_specs=[
            pl.BlockSpec((tq, D_ATT), row_map(COL_Q)),
            pl.BlockSpec((t_len, D_ATT), lambda b, t: (b, col0 + COL_K)),
            pl.BlockSpec((t_len, D_ATT), lambda b, t: (b, col0 + COL_V)),
            pl.BlockSpec((tq, D_ATT), row_map(COL_AG)),
            pl.BlockSpec((tq, D_ATT), row_map(COL_MQ)),
            pl.BlockSpec((tq, D_ATT), row_map(COL_MG)),
            pl.BlockSpec((1, N_MEM, D_ATT), lambda b, t: (b, 0, 0)),
            pl.BlockSpec((1, N_MEM, D_ATT), lambda b, t: (b, 0, 0)),
            pl.BlockSpec(bias.shape, lambda b, t: (0, 0, 0, 0)),
        ],
        out_specs=pl.BlockSpec((tq, 2 * D_ATT), lambda b, t: (b * n_t + t, 0)),
        out_shape=jax.ShapeDtypeStruct((n_seq * t_len, 2 * D_ATT), BF16),
        compiler_params=_compiler_params(2),
        name="attn_prompt",
    )(h, h, h, h, h, h, mk, mv, bias)


def _attn_sample_kernel(q_ref, k_ref, v_ref, ag_ref, mq_ref, mg_ref, kc_ref, vc_ref, mk_ref, mv_ref,
                        bias_ref, o_ref):
    heads = range(N_HEADS)
    ks = [jnp.concatenate([kc_ref[0, :, _head_cols(h)], k_ref[:, _head_cols(h)]], axis=0) for h in heads]
    vs = [jnp.concatenate([vc_ref[0, :, _head_cols(h)], v_ref[:, _head_cols(h)]], axis=0) for h in heads]
    os = _attend([q_ref[:, _head_cols(h)] for h in heads], ks, vs, [bias_ref[h] for h in heads])
    for h in heads:
        o_ref[:, _head_cols(h)] = (os[h] * ag_ref[:, _head_cols(h)].astype(F32)).astype(BF16)
    _memory_attention(mq_ref, mk_ref, mv_ref, mg_ref, o_ref)


def _attn_sample(h, kc, vc, mk, mv, bias, *, n_seq):
    def row_map(col):
        return lambda b: (b, col)

    seq_spec = lambda n: pl.BlockSpec((1, n, D_ATT), lambda b: (b, 0, 0))
    return pl.pallas_call(
        _attn_sample_kernel,
        grid=(n_seq,),
        in_specs=[
            pl.BlockSpec((CHUNK, D_ATT), row_map(COL_Q)),
            pl.BlockSpec((CHUNK, D_ATT), row_map(COL_K)),
            pl.BlockSpec((CHUNK, D_ATT), row_map(COL_V)),
            pl.BlockSpec((CHUNK, D_ATT), row_map(COL_AG)),
            pl.BlockSpec((CHUNK, D_ATT), row_map(COL_MQ)),
            pl.BlockSpec((CHUNK, D_ATT), row_map(COL_MG)),
            seq_spec(ATT_WIN), seq_spec(ATT_WIN), seq_spec(N_MEM), seq_spec(N_MEM),
            pl.BlockSpec(bias.shape, lambda b: (0, 0, 0)),
        ],
        out_specs=pl.BlockSpec((CHUNK, 2 * D_ATT), lambda b: (b, 0)),
        out_shape=jax.ShapeDtypeStruct((n_seq * CHUNK, 2 * D_ATT), BF16),
        compiler_params=_compiler_params(1),
        name="attn_sample",
    )(h, h, h, h, h, h, kc, vc, mk, mv, bias)


OUT_COLS = 512


def _out_proj_kernel(cm_ref, am_ref, x_ref, w_ref, gb_ref, y_ref, z_ref, *, alpha):
    tm = x_ref.shape[0]
    row_sum = jnp.zeros((tm, 1), F32)
    for n in range(D_MODEL // OUT_COLS):
        cols = slice(n * OUT_COLS, (n + 1) * OUT_COLS)
        z = jnp.dot(cm_ref[...], w_ref[0, 0:D_CONV, cols], preferred_element_type=F32)
        z = z + jnp.dot(am_ref[...], w_ref[0, D_CONV:D_MIX, cols], preferred_element_type=F32)
        z = alpha * x_ref[:, cols] + z
        z_ref[:, cols] = z
        row_sum = row_sum + jnp.sum(z, axis=-1, keepdims=True)
    mu = row_sum * (1.0 / D_MODEL)
    sq_sum = jnp.zeros((tm, 1), F32)
    for n in range(D_MODEL // OUT_COLS):
        cols = slice(n * OUT_COLS, (n + 1) * OUT_COLS)
        d = z_ref[:, cols] - mu
        sq_sum = sq_sum + jnp.sum(d * d, axis=-1, keepdims=True)
    inv = lax.rsqrt(sq_sum * (1.0 / D_MODEL) + LN_EPS)
    for n in range(D_MODEL // OUT_COLS):
        cols = slice(n * OUT_COLS, (n + 1) * OUT_COLS)
        y_ref[:, cols] = (z_ref[:, cols] - mu) * inv * gb_ref[0:1, cols] + gb_ref[1:2, cols]


def _out_proj(cm, am, x, w, layer, gb, *, alpha):
    m = x.shape[0]
    tm = min(512, m)
    assert m % tm == 0
    return pl.pallas_call(
        functools.partial(_out_proj_kernel, alpha=alpha),
        grid=(m // tm,),
        in_specs=[
            pl.BlockSpec((tm, D_CONV), lambda i: (i, 0)),
            pl.BlockSpec((tm, 2 * D_ATT), lambda i: (i, 0)),
            pl.BlockSpec((tm, D_MODEL), lambda i: (i, 0)),
            pl.BlockSpec((1, D_MIX, D_MODEL), lambda i: (layer, 0, 0)),
            pl.BlockSpec((8, D_MODEL), lambda i: (0, 0)),
        ],
        out_specs=pl.BlockSpec((tm, D_MODEL), lambda i: (i, 0)),
        out_shape=jax.ShapeDtypeStruct((m, D_MODEL), F32),
        scratch_shapes=[pltpu.VMEM((tm, D_MODEL), F32)],
        compiler_params=_compiler_params(1),
        name="out_proj",
    )(cm, am, x, w, gb)


OUT_ROWS = 256


def _out_proj_conv_kernel(yc_ref, g_ref, am_ref, x_ref, cp_ref, wpw_ref, w_ref, gb_ref, y_ref, *, alpha):
    tm = x_ref.shape[0]
    n_rows = min(OUT_ROWS, tm)
    for r0 in range(0, tm, n_rows):
        rows = slice(r0, r0 + n_rows)
        yn = _layer_norm(yc_ref[rows, :], cp_ref[1:2, :], cp_ref[2:3, :])
        sw = yn * _sigmoid(yn)
        pw = jnp.dot(sw.astype(BF16), wpw_ref[0], preferred_element_type=F32)
        cm = (pw * g_ref[rows, :].astype(F32)).astype(BF16)
        z = jnp.dot(cm, w_ref[0, 0:D_CONV, :], preferred_element_type=F32)
        z = z + jnp.dot(am_ref[rows, :], w_ref[0, D_CONV:D_MIX, :], preferred_element_type=F32)
        z = z + alpha * x_ref[rows, :]
        y_ref[rows, :] = _layer_norm(z, gb_ref[0:1, :], gb_ref[1:2, :])


def _out_proj_conv(yc, h, am, x, cp, wpw, w, layer, gb, *, alpha):
    m = x.shape[0]
    tm = min(512, m)
    assert m % tm == 0 and tm % min(OUT_ROWS, tm) == 0
    rows_of = lambda width: pl.BlockSpec((tm, width), lambda i: (i, 0))
    whole = lambda shape: pl.BlockSpec(shape, lambda i: (0, 0), pipeline_mode=ONCE)
    of_layer = lambda shape: pl.BlockSpec((1,) + shape, lambda i: (layer, 0, 0), pipeline_mode=ONCE)
    return pl.pallas_call(
        functools.partial(_out_proj_conv_kernel, alpha=alpha),
        grid=(m // tm,),
        in_specs=[
            rows_of(D_CONV), rows_of(D_CONV), rows_of(2 * D_ATT), rows_of(D_MODEL),
            whole((8, D_CONV)), of_layer((D_CONV, D_CONV)), of_layer((D_MIX, D_MODEL)), whole((8, D_MODEL)),
        ],
        out_specs=rows_of(D_MODEL),
        out_shape=jax.ShapeDtypeStruct((m, D_MODEL), F32),
        compiler_params=_compiler_params(1),
        name="out_proj_conv",
    )(yc, h, am, x, cp, wpw, w, gb)


def _band_bias(table):
    n_h = table.shape[0]
    n_u = Q_TILE + Q_BAND - 1
    n_flat = ATT_WIN + Q_TILE - REL_CLIP
    lo = REL_CLIP - (Q_TILE - 1)
    assert lo >= 0 and n_flat + 2 * REL_CLIP - lo == n_u
    u = jnp.concatenate([jnp.broadcast_to(table[:, 2 * REL_CLIP:], (n_h, n_flat)),
                         jnp.flip(table[:, lo:2 * REL_CLIP], axis=1)], axis=1).astype(F32)
    u = jnp.pad(u, ((0, 0), (0, 1)))
    skew = jnp.broadcast_to(u[:, None, :], (n_h, Q_TILE, n_u + 1)).reshape(n_h, Q_TILE * (n_u + 1))
    full = skew[:, Q_TILE - 1:Q_TILE - 1 + Q_TILE * n_u].reshape(n_h, Q_TILE, n_u)[:, :, :Q_BAND]
    qc = jnp.arange(Q_TILE)[:, None] // CHUNK
    kc = jnp.arange(Q_BAND)[None, :] // CHUNK
    full = jnp.where((kc >= qc) & (kc <= qc + N_LEFT_CHUNKS), full, MASK_VALUE)
    variants = []
    for v in range(N_BIAS_VARIANTS - 1):
        off = ATT_WIN - v * Q_TILE
        pad = jnp.full((n_h, Q_TILE, off), MASK_VALUE, F32)
        variants.append(jnp.concatenate([full[:, :, off:], pad], axis=-1))
    variants.append(full)
    return jnp.stack(variants)


def _pack_rows(rows, width):
    return jnp.concatenate([jnp.stack(rows).astype(F32), jnp.zeros((8 - len(rows), width), F32)])


def kernel(x_prompt, x_sample, mem_prompt, cache_conv, cache_att_k, cache_att_v, cache_mem_k, cache_mem_v,
           w_in, conv_w, conv_b, conv_ln_g, conv_ln_b, w_pw, rel_table, w_mem_kv, w_out, ln_g, ln_b):
    n_p, t_p, _ = x_prompt.shape
    n_s, t_s, _ = x_sample.shape
    depth = w_in.shape[0]
    n_mem = mem_prompt.shape[1]
    assert t_s == CHUNK and n_mem == N_MEM and cache_att_k.shape[2] == ATT_WIN
    alpha = (2 * depth) ** 0.25
    keep = min(ATT_WIN, t_p)

    w_in_b = w_in.astype(BF16)
    w_pw_b = w_pw.astype(BF16)
    w_out_b = w_out.astype(BF16)
    w_mkv_b = w_mem_kv.astype(BF16)

    mk_p, mv_p, mk_pb, mv_pb = _mem_kv(mem_prompt.reshape(n_p * n_mem, D_MODEL).astype(BF16), w_mkv_b)
    mk_pb = mk_pb.reshape(depth, n_p, n_mem, D_ATT)
    mv_pb = mv_pb.reshape(depth, n_p, n_mem, D_ATT)

    xp = x_prompt.reshape(n_p * t_p, D_MODEL)
    xs = x_sample.reshape(n_s * t_s, D_MODEL)
    zero_ctx = jnp.zeros((n_p, HALO, D_CONV), F32)
    cache_ctx = jnp.pad(cache_conv, ((0, 0), (0, 0), (HALO - CONV_LEFT, 0), (0, 0)))
    kc_b = cache_att_k.reshape(depth, n_s, ATT_WIN, D_ATT).astype(BF16)
    vc_b = cache_att_v.reshape(depth, n_s, ATT_WIN, D_ATT).astype(BF16)
    mk_sb = cache_mem_k.reshape(depth, n_s, n_mem, D_ATT).astype(BF16)
    mv_sb = cache_mem_v.reshape(depth, n_s, n_mem, D_ATT).astype(BF16)

    conv_p, kp_l, vp_l, conv_s, ks_l, vs_l = [], [], [], [], [], []
    for l in range(depth):
        cw = jnp.broadcast_to(conv_w[l][:, None, :], (CONV_W, SUBLANES, D_CONV))
        cp = _pack_rows([conv_b[l], conv_ln_g[l], conv_ln_b[l]], D_CONV)
        gb = _pack_rows([ln_g[l], ln_b[l]], D_MODEL)
        bias = _band_bias(rel_table[l])

        h, yc, cs, k_new, v_new = _in_proj_conv(xp, w_in_b, l, zero_ctx, cw, cp, seq_rows=t_p, keep=keep)
        am = _attn_prompt(h, mk_pb[l], mv_pb[l], bias, n_seq=n_p, t_len=t_p, tq=512, col0=-N_GLU_BLOCKS)
        xp = _out_proj_conv(yc, h, am, xp, cp, w_pw_b, w_out_b, l, gb, alpha=alpha)
        conv_p.append(cs)
        kp_l.append(k_new)
        vp_l.append(v_new)

        h, k_new, v_new = _in_proj(xs, w_in_b, l, seq_rows=n_s * t_s, keep=n_s * t_s)
        cm, cs = _conv_branch(h, cache_ctx[l], cw, cp, w_pw_b, l, n_seq=n_s, t_len=t_s, tq=t_s)
        am = _attn_sample(h, kc_b[l], vc_b[l], mk_sb[l], mv_sb[l], bias[-1, :, :CHUNK, :BAND], n_seq=n_s)
        xs = _out_proj(cm, am, xs, w_out_b, l, gb, alpha=alpha)
        conv_s.append(cs)
        ks_l.append(k_new)
        vs_l.append(v_new)

    mem_shape = (depth, n_p, n_mem, N_HEADS, HEAD_DIM)
    kv_p_shape = (depth, n_p, keep, N_HEADS, HEAD_DIM)
    kv_s_shape = (depth, n_s, t_s, N_HEADS, HEAD_DIM)
    return (xp.reshape(n_p, t_p, D_MODEL), xs.reshape(n_s, t_s, D_MODEL), jnp.stack(conv_p),
            jnp.stack(kp_l).reshape(kv_p_shape), jnp.stack(vp_l).reshape(kv_p_shape),
            mk_p.reshape(mem_shape), mv_p.reshape(mem_shape),
            jnp.stack(conv_s), jnp.stack(ks_l).reshape(kv_s_shape), jnp.stack(vs_l).reshape(kv_s_shape))
```

```python
import functools

import jax
import jax.numpy as jnp
from jax import lax
from jax.experimental import pallas as pl
from jax.experimental.pallas import tpu as pltpu

D_MODEL = 2048
CHUNK = 64
N_LEFT_CHUNKS = 8
ATT_WIN = N_LEFT_CHUNKS * CHUNK
BAND = ATT_WIN + CHUNK
CONV_W = 31
CONV_LEFT = CONV_W - 1
D_CONV = D_MODEL // 2
HEAD_DIM = 128
N_HEADS = 4
D_ATT = N_HEADS * HEAD_DIM
N_MEM = 256
REL_CLIP = 128
D_MIX = D_CONV + 2 * D_ATT
D_IN = 3 * D_CONV + 6 * D_ATT
LN_EPS = 1e-5
SM_SCALE = HEAD_DIM ** -0.5
MASK_VALUE = -1e30

COL_Q, COL_K, COL_V, COL_AG, COL_MQ, COL_MG = 6, 7, 8, 9, 10, 11
N_COL_BLOCKS = D_IN // D_ATT
N_GLU_BLOCKS = 4

LANES = 128
SUBLANES = 8
HALO = 32

Q_CHUNKS = 2
Q_TILE = Q_CHUNKS * CHUNK
Q_BAND = ATT_WIN + Q_TILE
N_BIAS_VARIANTS = N_LEFT_CHUNKS // Q_CHUNKS + 1

BF16 = jnp.bfloat16
F32 = jnp.float32

VMEM_LIMIT_BYTES = 56 * 1024 * 1024
ONCE = pl.Buffered(1)


def _compiler_params(n_axes):
    return pltpu.CompilerParams(
        dimension_semantics=("arbitrary",) * n_axes, vmem_limit_bytes=VMEM_LIMIT_BYTES)


def _sigmoid(x):
    return 1.0 / (1.0 + jnp.exp(-x))


def _layer_norm(z, g, b):
    mu = jnp.mean(z, axis=-1, keepdims=True)
    d = z - mu
    var = jnp.mean(d * d, axis=-1, keepdims=True)
    return d * lax.rsqrt(var + LN_EPS) * g + b


IN_PROJ_ROWS = 128


def _store_tail(tail_ref, acc, r0, tm):
    keep = tail_ref.shape[0]
    n = acc.shape[0]
    if r0 + n > tm - keep:
        lo = max(r0, tm - keep)
        tail_ref[lo - (tm - keep):r0 + n - (tm - keep), :] = acc[lo - r0:, :]


def _in_proj_kernel(x_ref, w_ref, h_ref, k_ref, v_ref, xb_ref, tail_ref):
    j = pl.program_id(1)
    tm = x_ref.shape[0]

    @pl.when(j == 0)
    def _():
        xb_ref[...] = x_ref[...].astype(BF16)

    is_sig = (j == 2) | (j == 3)
    is_silu = (j == 4) | (j == 5) | (j == COL_AG) | (j == COL_MG)
    n_rows = min(IN_PROJ_ROWS, tm)
    for r0 in range(0, tm, n_rows):
        rows = slice(r0, r0 + n_rows)
        acc = jnp.dot(xb_ref[rows, :], w_ref[0], preferred_element_type=F32)
        sg = _sigmoid(acc)
        out = jnp.where(is_sig, sg, jnp.where(is_silu, acc * sg, acc))
        h_ref[rows, :] = out.astype(BF16)
        _store_tail(tail_ref, acc, r0, tm)

    @pl.when(j == COL_K)
    def _():
        k_ref[...] = tail_ref[...]

    @pl.when(j == COL_V)
    def _():
        v_ref[...] = tail_ref[...]


def _in_proj(x, w, layer, *, seq_rows, keep):
    m = x.shape[0]
    tm = min(1024, m)
    assert m % tm == 0 and seq_rows % tm == 0 and keep <= tm and tm % min(IN_PROJ_ROWS, tm) == 0
    tiles_per_seq = seq_rows // tm
    n_seq = m // seq_rows
    keep_spec = pl.BlockSpec((keep, D_ATT), lambda i, j: (i // tiles_per_seq, 0))
    return pl.pallas_call(
        _in_proj_kernel,
        grid=(m // tm, N_COL_BLOCKS),
        in_specs=[
            pl.BlockSpec((tm, D_MODEL), lambda i, j: (i, 0)),
            pl.BlockSpec((1, D_MODEL, D_ATT), lambda i, j: (layer, 0, j)),
        ],
        out_specs=[pl.BlockSpec((tm, D_ATT), lambda i, j: (i, j)), keep_spec, keep_spec],
        out_shape=[
            jax.ShapeDtypeStruct((m, D_IN), BF16),
            jax.ShapeDtypeStruct((n_seq * keep, D_ATT), F32),
            jax.ShapeDtypeStruct((n_seq * keep, D_ATT), F32),
        ],
        scratch_shapes=[pltpu.VMEM((tm, D_MODEL), BF16), pltpu.VMEM((keep, D_ATT), F32)],
        compiler_params=_compiler_params(2),
        name="in_proj",
    )(x, w)


TAP_ROWS = 32


def _conv_offsets(s):
    return [o for o in range(HALO - CONV_LEFT, HALO + 1) if o % SUBLANES == s]


def _in_proj_conv_kernel(x_ref, w_ref, ctx_ref, cw_ref, cb_ref, h_ref, y_ref, st_ref, k_ref, v_ref,
                         xb_ref, tail_ref, up_ref, *, tiles_per_seq):
    i = pl.program_id(0)
    j = pl.program_id(1)
    tm = x_ref.shape[0]
    n_rows = min(IN_PROJ_ROWS, tm)
    seq_start = i % tiles_per_seq == 0

    @pl.when(j == 0)
    def _():
        xb_ref[...] = x_ref[...].astype(BF16)

    @pl.when((j == 0) & seq_start)
    def _():
        up_ref[0:HALO, :] = ctx_ref[0]

    @pl.when((j == 0) & jnp.logical_not(seq_start))
    def _():
        up_ref[0:HALO, :] = up_ref[tm:tm + HALO, :]

    def matmul_blocks(epilogue):
        for r0 in range(0, tm, n_rows):
            acc = jnp.dot(xb_ref[r0:r0 + n_rows, :], w_ref[0], preferred_element_type=F32)
            epilogue(r0, acc)

    for jj in range(N_GLU_BLOCKS):
        cols = slice((jj % 2) * D_ATT, (jj % 2 + 1) * D_ATT)

        @pl.when(j == jj)
        def _(jj=jj, cols=cols):
            def glu(r0, acc):
                rows = slice(HALO + r0, HALO + r0 + n_rows)
                if jj < 2:
                    up_ref[rows, cols] = acc
                else:
                    up_ref[rows, cols] = up_ref[rows, cols] * _sigmoid(acc)
            matmul_blocks(glu)

    @pl.when(j == N_GLU_BLOCKS)
    def _():
        st_ref[0] = up_ref[HALO + tm - CONV_LEFT:HALO + tm, :]

    @pl.when(j >= N_GLU_BLOCKS)
    def _():
        is_silu = (j == 4) | (j == 5) | (j == COL_AG) | (j == COL_MG)

        lane = pl.ds(pl.multiple_of((j - N_GLU_BLOCKS) * LANES, LANES), LANES)
        groups = TAP_ROWS // SUBLANES

        def conv_rows(r0):
            base = up_ref[r0:r0 + HALO + TAP_ROWS, lane]
            parts = [None, None]
            for s in range(SUBLANES):
                offs = _conv_offsets(s)
                lo = offs[0] - s
                win = base[s + lo:offs[-1] + TAP_ROWS]
                for o in offs:
                    tap = win[o - s - lo:o - s - lo + TAP_ROWS]
                    w = cw_ref[o - (HALO - CONV_LEFT), :, lane]
                    term = tap.reshape(groups, SUBLANES, LANES) * w[None]
                    parts[s % 2] = term if parts[s % 2] is None else parts[s % 2] + term
            acc = (parts[0] + parts[1]).reshape(TAP_ROWS, LANES)
            y_ref[r0:r0 + TAP_ROWS, lane] = acc + cb_ref[0:1, lane]

        n_mm = tm // n_rows
        n_tap_blocks = tm // TAP_ROWS
        weights = [3] * (n_mm - 1) + [1]
        bounds = [n_tap_blocks * sum(weights[:r]) // sum(weights) for r in range(n_mm)] + [n_tap_blocks]
        for r in range(n_mm):
            r0 = r * n_rows
            acc = jnp.dot(xb_ref[r0:r0 + n_rows, :], w_ref[0], preferred_element_type=F32)
            for blk in range(bounds[r], bounds[r + 1]):
                conv_rows(blk * TAP_ROWS)
            out = jnp.where(is_silu, acc * _sigmoid(acc), acc)
            h_ref[r0:r0 + n_rows, :] = out.astype(BF16)
            _store_tail(tail_ref, acc, r0, tm)

    @pl.when(j == COL_K)
    def _():
        k_ref[...] = tail_ref[...]

    @pl.when(j == COL_V)
    def _():
        v_ref[...] = tail_ref[...]


def _in_proj_conv(x, w, layer, ctx, cw, cb, *, seq_rows, keep):
    m = x.shape[0]
    tm = min(1024, m)
    assert m % tm == 0 and seq_rows % tm == 0 and keep <= tm and tm % min(IN_PROJ_ROWS, tm) == 0
    assert tm % TAP_ROWS == 0 and N_COL_BLOCKS - N_GLU_BLOCKS == D_CONV // LANES
    tiles_per_seq = seq_rows // tm
    n_seq = m // seq_rows
    seq_of = lambda i, j: i // tiles_per_seq
    keep_spec = pl.BlockSpec((keep, D_ATT), lambda i, j: (seq_of(i, j), 0))
    return pl.pallas_call(
        functools.partial(_in_proj_conv_kernel, tiles_per_seq=tiles_per_seq),
        grid=(m // tm, N_COL_BLOCKS),
        in_specs=[
            pl.BlockSpec((tm, D_MODEL), lambda i, j: (i, 0)),
            pl.BlockSpec((1, D_MODEL, D_ATT), lambda i, j: (layer, 0, j)),
            pl.BlockSpec((1, HALO, D_CONV), lambda i, j: (seq_of(i, j), 0, 0)),
            pl.BlockSpec((CONV_W, SUBLANES, D_CONV), lambda i, j: (0, 0, 0), pipeline_mode=ONCE),
            pl.BlockSpec((8, D_CONV), lambda i, j: (0, 0), pipeline_mode=ONCE),
        ],
        out_specs=[
            pl.BlockSpec((tm, D_ATT), lambda i, j: (i, jnp.maximum(j - N_GLU_BLOCKS, 0))),
            pl.BlockSpec((tm, D_CONV), lambda i, j: (i, 0)),
            pl.BlockSpec((1, CONV_LEFT, D_CONV), lambda i, j: (seq_of(i, j), 0, 0)),
            keep_spec, keep_spec,
        ],
        out_shape=[
            jax.ShapeDtypeStruct((m, D_IN - N_GLU_BLOCKS * D_ATT), BF16),
            jax.ShapeDtypeStruct((m, D_CONV), F32),
            jax.ShapeDtypeStruct((n_seq, CONV_LEFT, D_CONV), F32),
            jax.ShapeDtypeStruct((n_seq * keep, D_ATT), F32),
            jax.ShapeDtypeStruct((n_seq * keep, D_ATT), F32),
        ],
        scratch_shapes=[
            pltpu.VMEM((tm, D_MODEL), BF16),
            pltpu.VMEM((keep, D_ATT), F32),
            pltpu.VMEM((HALO + tm, D_CONV), F32),
        ],
        compiler_params=pltpu.CompilerParams(
            dimension_semantics=("arbitrary", "arbitrary"), vmem_limit_bytes=60 * 1024 * 1024),
        name="in_proj_conv",
    )(x, w, ctx, cw, cb)


def _mem_kv_kernel(x_ref, w_ref, k_ref, v_ref, kb_ref, vb_ref):
    acc = jnp.dot(x_ref[...], w_ref[0], preferred_element_type=F32)
    k = acc[:, :D_ATT]
    v = acc[:, D_ATT:]
    k_ref[0] = k
    v_ref[0] = v
    kb_ref[0] = k.astype(BF16)
    vb_ref[0] = v.astype(BF16)


def _mem_kv(mem, w):
    m = mem.shape[0]
    depth = w.shape[0]
    tm = min(1024, m)
    assert m % tm == 0
    out_spec = pl.BlockSpec((1, tm, D_ATT), lambda i, l: (l, i, 0))
    return pl.pallas_call(
        _mem_kv_kernel,
        grid=(m // tm, depth),
        in_specs=[
            pl.BlockSpec((tm, D_MODEL), lambda i, l: (i, 0)),
            pl.BlockSpec((1, D_MODEL, 2 * D_ATT), lambda i, l: (l, 0, 0)),
        ],
        out_specs=[out_spec] * 4,
        out_shape=[jax.ShapeDtypeStruct((depth, m, D_ATT), F32)] * 2
        + [jax.ShapeDtypeStruct((depth, m, D_ATT), BF16)] * 2,
        compiler_params=_compiler_params(2),
        name="mem_kv",
    )(mem, w)


CONV_ROWS = 32
CONV_LANES = 256


def _conv_kernel(a_ref, sb_ref, g_ref, ah_ref, sbh_ref, ctx_ref, cw_ref, cp_ref, wpw_ref,
                 o_ref, st_ref, up_ref, us_ref, y_ref, *, tq, n_t):
    t = pl.program_id(1)

    up_ref[HALO:HALO + tq, :] = a_ref[...].astype(F32) * sb_ref[...].astype(F32)

    @pl.when(t == 0)
    def _():
        up_ref[0:HALO, :] = ctx_ref[0]

    @pl.when(t > 0)
    def _():
        up_ref[0:HALO, :] = ah_ref[...].astype(F32) * sbh_ref[...].astype(F32)

    n_shifted = tq + HALO - SUBLANES
    for s in range(1, SUBLANES):
        us_ref[s - 1, 0:n_shifted, :] = up_ref[s:s + n_shifted, :]

    groups = CONV_ROWS // SUBLANES
    for lc in range(D_CONV // CONV_LANES):
        lanes = slice(lc * CONV_LANES, (lc + 1) * CONV_LANES)

        def row_block(rb, carry, lanes=lanes):
            r0 = pl.multiple_of(rb * CONV_ROWS, CONV_ROWS)
            parts = [None, None]
            for s in range(SUBLANES):
                offs = _conv_offsets(s)
                lo = offs[0] - s
                rows = pl.ds(r0 + lo, offs[-1] - s + CONV_ROWS - lo)
                win = up_ref[rows, lanes] if s == 0 else us_ref[s - 1, rows, lanes]
                for o in offs:
                    tap = win[o - s - lo:o - s - lo + CONV_ROWS]
                    w = cw_ref[o - (HALO - CONV_LEFT), :, lanes]
                    term = tap.reshape(groups, SUBLANES, CONV_LANES) * w[None]
                    parts[s % 2] = term if parts[s % 2] is None else parts[s % 2] + term
            acc = parts[0] + parts[1]
            y_ref[pl.ds(r0, CONV_ROWS), lanes] = acc.reshape(CONV_ROWS, CONV_LANES) + cp_ref[0:1, lanes]
            return carry

        lax.fori_loop(0, tq // CONV_ROWS, row_block, 0)

    yn = _layer_norm(y_ref[...], cp_ref[1:2, :], cp_ref[2:3, :])
    sw = yn * _sigmoid(yn)
    pw = jnp.dot(sw.astype(BF16), wpw_ref[0], preferred_element_type=F32)
    o_ref[...] = (pw * g_ref[...].astype(F32)).astype(BF16)

    @pl.when(t == n_t - 1)
    def _():
        st_ref[0] = up_ref[HALO + tq - CONV_LEFT:HALO + tq, :]


def _conv_branch(h, ctx, cw, cp, wpw, layer, *, n_seq, t_len, tq):
    assert t_len % tq == 0 and tq % HALO == 0 and tq >= HALO
    n_t = t_len // tq
    m = n_seq * t_len

    def row_map(col):
        return lambda b, t: (b * n_t + t, col)

    def halo_map(col):
        return lambda b, t: (jnp.maximum((b * t_len + t * tq) // HALO - 1, 0), col)

    return pl.pallas_call(
        functools.partial(_conv_kernel, tq=tq, n_t=n_t),
        grid=(n_seq, n_t),
        in_specs=[
            pl.BlockSpec((tq, D_CONV), row_map(0)),
            pl.BlockSpec((tq, D_CONV), row_map(1)),
            pl.BlockSpec((tq, D_CONV), row_map(2)),
            pl.BlockSpec((HALO, D_CONV), halo_map(0)),
            pl.BlockSpec((HALO, D_CONV), halo_map(1)),
            pl.BlockSpec((1, HALO, D_CONV), lambda b, t: (b, 0, 0)),
            pl.BlockSpec((CONV_W, SUBLANES, D_CONV), lambda b, t: (0, 0, 0)),
            pl.BlockSpec((8, D_CONV), lambda b, t: (0, 0)),
            pl.BlockSpec((1, D_CONV, D_CONV), lambda b, t: (layer, 0, 0)),
        ],
        out_specs=[
            pl.BlockSpec((tq, D_CONV), lambda b, t: (b * n_t + t, 0)),
            pl.BlockSpec((1, CONV_LEFT, D_CONV), lambda b, t: (b, 0, 0)),
        ],
        out_shape=[
            jax.ShapeDtypeStruct((m, D_CONV), BF16),
            jax.ShapeDtypeStruct((n_seq, CONV_LEFT, D_CONV), F32),
        ],
        scratch_shapes=[
            pltpu.VMEM((HALO + tq, D_CONV), F32),
            pltpu.VMEM((SUBLANES - 1, HALO + tq, D_CONV), F32),
            pltpu.VMEM((tq, D_CONV), F32),
        ],
        compiler_params=_compiler_params(2),
        name="conv_branch",
    )(h, h, h, h, h, ctx, cw, cp, wpw)


def _qk(q, k):
    return lax.dot_general(q, k, (((1,), (1,)), ((), ())), preferred_element_type=F32)


def _attend(qs, ks, vs, biases):
    ss = [_qk(q, k) * SM_SCALE for q, k in zip(qs, ks)]
    ss = [s if b is None else s + b for s, b in zip(ss, biases)]
    ps, ls = [], []
    for s in ss:
        m = jnp.max(s, axis=-1, keepdims=True)
        p = jnp.exp(s - m)
        ls.append(jnp.sum(p, axis=-1, keepdims=True))
        ps.append(p.astype(BF16))
    return [jnp.dot(p, v, preferred_element_type=F32) / l for p, v, l in zip(ps, vs, ls)]


def _head_cols(h):
    return slice(h * HEAD_DIM, (h + 1) * HEAD_DIM)


def _memory_attention(mq_ref, mk_ref, mv_ref, mg_ref, o_ref):
    heads = range(N_HEADS)
    os = _attend([mq_ref[:, _head_cols(h)] for h in heads], [mk_ref[0, 0, :, _head_cols(h)] for h in heads],
                 [mv_ref[0, 0, :, _head_cols(h)] for h in heads], [None] * N_HEADS)
    for h in heads:
        o_ref[:, D_ATT + h * HEAD_DIM:D_ATT + (h + 1) * HEAD_DIM] = (
            os[h] * mg_ref[:, _head_cols(h)].astype(F32)).astype(BF16)


def _attn_prompt_kernel(q_ref, k_ref, v_ref, ag_ref, mq_ref, mg_ref, mk_ref, mv_ref, bias_ref,
                        o_ref, *, tq):
    t = pl.program_id(1)
    n_q = tq // Q_TILE
    heads = range(N_HEADS)

    def q_tile(qi, carry):
        c0 = (t * n_q + qi) * Q_CHUNKS
        start = pl.multiple_of(jnp.maximum(c0 * CHUNK - ATT_WIN, 0), Q_TILE)
        variant = jnp.minimum(c0 // Q_CHUNKS, N_BIAS_VARIANTS - 1)
        rows = pl.ds(pl.multiple_of(qi * Q_TILE, Q_TILE), Q_TILE)
        band = pl.ds(start, Q_BAND)
        os = _attend([q_ref[rows, _head_cols(h)] for h in heads], [k_ref[band, _head_cols(h)] for h in heads],
                     [v_ref[band, _head_cols(h)] for h in heads], [bias_ref[variant, h] for h in heads])
        for h in heads:
            o_ref[rows, _head_cols(h)] = (os[h] * ag_ref[rows, _head_cols(h)].astype(F32)).astype(BF16)
        return carry

    lax.fori_loop(0, n_q, q_tile, 0)
    _memory_attention(mq_ref, mk_ref, mv_ref, mg_ref, o_ref)


def _attn_prompt(h, mk, mv, layer, bias, *, n_seq, t_len, tq, col0):
    assert t_len % tq == 0 and tq % Q_TILE == 0 and t_len >= Q_BAND
    n_t = t_len // tq

    def row_map(col):
        return lambda b, t: (b * n_t + t, col0 + col)

    return pl.pallas_call(
        functools.partial(_attn_prompt_kernel, tq=tq),
        grid=(n_seq, n_t),
        in_specs=[
            pl.BlockSpec((tq, D_ATT), row_map(COL_Q)),
            pl.BlockSpec((t_len, D_ATT), lambda b, t: (b, col0 + COL_K)),
            pl.BlockSpec((t_len, D_ATT), lambda b, t: (b, col0 + COL_V)),
            pl.BlockSpec((tq, D_ATT), row_map(COL_AG)),
            pl.BlockSpec((tq, D_ATT), row_map(COL_MQ)),
            pl.BlockSpec((tq, D_ATT), row_map(COL_MG)),
            pl.BlockSpec((1, 1, N_MEM, D_ATT), lambda b, t: (layer, b, 0, 0)),
            pl.BlockSpec((1, 1, N_MEM, D_ATT), lambda b, t: (layer, b, 0, 0)),
            pl.BlockSpec(bias.shape, lambda b, t: (0, 0, 0, 0)),
        ],
        out_specs=pl.BlockSpec((tq, 2 * D_ATT), lambda b, t: (b * n_t + t, 0)),
        out_shape=jax.ShapeDtypeStruct((n_seq * t_len, 2 * D_ATT), BF16),
        compiler_params=_compiler_params(2),
        name="attn_prompt",
    )(h, h, h, h, h, h, mk, mv, bias)


def _attn_sample_kernel(q_ref, k_ref, v_ref, ag_ref, mq_ref, mg_ref, kc_ref, vc_ref, mk_ref, mv_ref,
                        bias_ref, o_ref):
    heads = range(N_HEADS)
    ks = [jnp.concatenate([kc_ref[0, 0, :, _head_cols(h)], k_ref[:, _head_cols(h)]], axis=0) for h in heads]
    vs = [jnp.concatenate([vc_ref[0, 0, :, _head_cols(h)], v_ref[:, _head_cols(h)]], axis=0) for h in heads]
    os = _attend([q_ref[:, _head_cols(h)] for h in heads], ks, vs, [bias_ref[h] for h in heads])
    for h in heads:
        o_ref[:, _head_cols(h)] = (os[h] * ag_ref[:, _head_cols(h)].astype(F32)).astype(BF16)
    _memory_attention(mq_ref, mk_ref, mv_ref, mg_ref, o_ref)


def _attn_sample(h, kc, vc, mk, mv, layer, bias, *, n_seq):
    def row_map(col):
        return lambda b: (b, col)

    seq_spec = lambda n: pl.BlockSpec((1, 1, n, D_ATT), lambda b: (layer, b, 0, 0))
    return pl.pallas_call(
        _attn_sample_kernel,
        grid=(n_seq,),
        in_specs=[
            pl.BlockSpec((CHUNK, D_ATT), row_map(COL_Q)),
            pl.BlockSpec((CHUNK, D_ATT), row_map(COL_K)),
            pl.BlockSpec((CHUNK, D_ATT), row_map(COL_V)),
            pl.BlockSpec((CHUNK, D_ATT), row_map(COL_AG)),
            pl.BlockSpec((CHUNK, D_ATT), row_map(COL_MQ)),
            pl.BlockSpec((CHUNK, D_ATT), row_map(COL_MG)),
            seq_spec(ATT_WIN), seq_spec(ATT_WIN), seq_spec(N_MEM), seq_spec(N_MEM),
            pl.BlockSpec(bias.shape, lambda b: (0, 0, 0)),
        ],
        out_specs=pl.BlockSpec((CHUNK, 2 * D_ATT), lambda b: (b, 0)),
        out_shape=jax.ShapeDtypeStruct((n_seq * CHUNK, 2 * D_ATT), BF16),
        compiler_params=_compiler_params(1),
        name="attn_sample",
    )(h, h, h, h, h, h, kc, vc, mk, mv, bias)


OUT_COLS = 512


def _out_proj_kernel(cm_ref, am_ref, x_ref, w_ref, gb_ref, y_ref, z_ref, *, alpha):
    tm = x_ref.shape[0]
    row_sum = jnp.zeros((tm, 1), F32)
    for n in range(D_MODEL // OUT_COLS):
        cols = slice(n * OUT_COLS, (n + 1) * OUT_COLS)
        z = jnp.dot(cm_ref[...], w_ref[0, 0:D_CONV, cols], preferred_element_type=F32)
        z = z + jnp.dot(am_ref[...], w_ref[0, D_CONV:D_MIX, cols], preferred_element_type=F32)
        z = alpha * x_ref[:, cols] + z
        z_ref[:, cols] = z
        row_sum = row_sum + jnp.sum(z, axis=-1, keepdims=True)
    mu = row_sum * (1.0 / D_MODEL)
    sq_sum = jnp.zeros((tm, 1), F32)
    for n in range(D_MODEL // OUT_COLS):
        cols = slice(n * OUT_COLS, (n + 1) * OUT_COLS)
        d = z_ref[:, cols] - mu
        sq_sum = sq_sum + jnp.sum(d * d, axis=-1, keepdims=True)
    inv = lax.rsqrt(sq_sum * (1.0 / D_MODEL) + LN_EPS)
    for n in range(D_MODEL // OUT_COLS):
        cols = slice(n * OUT_COLS, (n + 1) * OUT_COLS)
        y_ref[:, cols] = (z_ref[:, cols] - mu) * inv * gb_ref[0:1, cols] + gb_ref[1:2, cols]


def _out_proj(cm, am, x, w, layer, gb, *, alpha):
    m = x.shape[0]
    tm = min(512, m)
    assert m % tm == 0
    return pl.pallas_call(
        functools.partial(_out_proj_kernel, alpha=alpha),
        grid=(m // tm,),
        in_specs=[
            pl.BlockSpec((tm, D_CONV), lambda i: (i, 0)),
            pl.BlockSpec((tm, 2 * D_ATT), lambda i: (i, 0)),
            pl.BlockSpec((tm, D_MODEL), lambda i: (i, 0)),
            pl.BlockSpec((1, D_MIX, D_MODEL), lambda i: (layer, 0, 0)),
            pl.BlockSpec((8, D_MODEL), lambda i: (0, 0)),
        ],
        out_specs=pl.BlockSpec((tm, D_MODEL), lambda i: (i, 0)),
        out_shape=jax.ShapeDtypeStruct((m, D_MODEL), F32),
        scratch_shapes=[pltpu.VMEM((tm, D_MODEL), F32)],
        compiler_params=_compiler_params(1),
        name="out_proj",
    )(cm, am, x, w, gb)


OUT_ROWS = 256


def _out_proj_conv_kernel(yc_ref, g_ref, am_ref, x_ref, cp_ref, wpw_ref, w_ref, gb_ref, y_ref, *, alpha):
    tm = x_ref.shape[0]
    n_rows = min(OUT_ROWS, tm)
    for r0 in range(0, tm, n_rows):
        rows = slice(r0, r0 + n_rows)
        yn = _layer_norm(yc_ref[rows, :], cp_ref[1:2, :], cp_ref[2:3, :])
        sw = yn * _sigmoid(yn)
        pw = jnp.dot(sw.astype(BF16), wpw_ref[0], preferred_element_type=F32)
        cm = (pw * g_ref[rows, :].astype(F32)).astype(BF16)
        z = jnp.dot(cm, w_ref[0, 0:D_CONV, :], preferred_element_type=F32)
        z = z + jnp.dot(am_ref[rows, :], w_ref[0, D_CONV:D_MIX, :], preferred_element_type=F32)
        z = z + alpha * x_ref[rows, :]
        y_ref[rows, :] = _layer_norm(z, gb_ref[0:1, :], gb_ref[1:2, :])


def _out_proj_conv(yc, h, am, x, cp, wpw, w, layer, gb, *, alpha):
    m = x.shape[0]
    tm = min(512, m)
    assert m % tm == 0 and tm % min(OUT_ROWS, tm) == 0
    rows_of = lambda width: pl.BlockSpec((tm, width), lambda i: (i, 0))
    whole = lambda shape: pl.BlockSpec(shape, lambda i: (0, 0), pipeline_mode=ONCE)
    of_layer = lambda shape: pl.BlockSpec((1,) + shape, lambda i: (layer, 0, 0), pipeline_mode=ONCE)
    return pl.pallas_call(
        functools.partial(_out_proj_conv_kernel, alpha=alpha),
        grid=(m // tm,),
        in_specs=[
            rows_of(D_CONV), rows_of(D_CONV), rows_of(2 * D_ATT), rows_of(D_MODEL),
            whole((8, D_CONV)), of_layer((D_CONV, D_CONV)), of_layer((D_MIX, D_MODEL)), whole((8, D_MODEL)),
        ],
        out_specs=rows_of(D_MODEL),
        out_shape=jax.ShapeDtypeStruct((m, D_MODEL), F32),
        compiler_params=_compiler_params(1),
        name="out_proj_conv",
    )(yc, h, am, x, cp, wpw, w, gb)


def _band_bias(table):
    n_h = table.shape[0]
    n_u = Q_TILE + Q_BAND - 1
    n_flat = ATT_WIN + Q_TILE - REL_CLIP
    lo = REL_CLIP - (Q_TILE - 1)
    assert lo >= 0 and n_flat + 2 * REL_CLIP - lo == n_u
    u = jnp.concatenate([jnp.broadcast_to(table[:, 2 * REL_CLIP:], (n_h, n_flat)),
                         jnp.flip(table[:, lo:2 * REL_CLIP], axis=1)], axis=1).astype(F32)
    u = jnp.pad(u, ((0, 0), (0, 1)))
    skew = jnp.broadcast_to(u[:, None, :], (n_h, Q_TILE, n_u + 1)).reshape(n_h, Q_TILE * (n_u + 1))
    full = skew[:, Q_TILE - 1:Q_TILE - 1 + Q_TILE * n_u].reshape(n_h, Q_TILE, n_u)[:, :, :Q_BAND]
    qc = jnp.arange(Q_TILE)[:, None] // CHUNK
    kc = jnp.arange(Q_BAND)[None, :] // CHUNK
    full = jnp.where((kc >= qc) & (kc <= qc + N_LEFT_CHUNKS), full, MASK_VALUE)
    variants = []
    for v in range(N_BIAS_VARIANTS - 1):
        off = ATT_WIN - v * Q_TILE
        pad = jnp.full((n_h, Q_TILE, off), MASK_VALUE, F32)
        variants.append(jnp.concatenate([full[:, :, off:], pad], axis=-1))
    variants.append(full)
    return jnp.stack(variants)


def _pack_rows(rows, width):
    return jnp.concatenate([jnp.stack(rows).astype(F32), jnp.zeros((8 - len(rows), width), F32)])


def kernel(x_prompt, x_sample, mem_prompt, cache_conv, cache_att_k, cache_att_v, cache_mem_k, cache_mem_v,
           w_in, conv_w, conv_b, conv_ln_g, conv_ln_b, w_pw, rel_table, w_mem_kv, w_out, ln_g, ln_b):
    n_p, t_p, _ = x_prompt.shape
    n_s, t_s, _ = x_sample.shape
    depth = w_in.shape[0]
    n_mem = mem_prompt.shape[1]
    assert t_s == CHUNK and n_mem == N_MEM and cache_att_k.shape[2] == ATT_WIN
    alpha = (2 * depth) ** 0.25
    keep = min(ATT_WIN, t_p)

    w_in_b = w_in.astype(BF16)
    w_pw_b = w_pw.astype(BF16)
    w_out_b = w_out.astype(BF16)
    w_mkv_b = w_mem_kv.astype(BF16)

    mk_p, mv_p, mk_pb, mv_pb = _mem_kv(mem_prompt.reshape(n_p * n_mem, D_MODEL).astype(BF16), w_mkv_b)
    mk_pb = mk_pb.reshape(depth, n_p, n_mem, D_ATT)
    mv_pb = mv_pb.reshape(depth, n_p, n_mem, D_ATT)

    xp = x_prompt.reshape(n_p * t_p, D_MODEL)
    xs = x_sample.reshape(n_s * t_s, D_MODEL)
    zero_ctx = jnp.zeros((n_p, HALO, D_CONV), F32)
    cache_ctx = jnp.pad(cache_conv, ((0, 0), (0, 0), (HALO - CONV_LEFT, 0), (0, 0)))
    kc_b = cache_att_k.reshape(depth, n_s, ATT_WIN, D_ATT).astype(BF16)
    vc_b = cache_att_v.reshape(depth, n_s, ATT_WIN, D_ATT).astype(BF16)
    mk_sb = cache_mem_k.reshape(depth, n_s, n_mem, D_ATT).astype(BF16)
    mv_sb = cache_mem_v.reshape(depth, n_s, n_mem, D_ATT).astype(BF16)

    conv_p, kp_l, vp_l, conv_s, ks_l, vs_l = [], [], [], [], [], []
    for l in range(depth):
        cw = jnp.broadcast_to(conv_w[l][:, None, :], (CONV_W, SUBLANES, D_CONV))
        cp = _pack_rows([conv_b[l], conv_ln_g[l], conv_ln_b[l]], D_CONV)
        gb = _pack_rows([ln_g[l], ln_b[l]], D_MODEL)
        bias = _band_bias(rel_table[l])

        h, yc, cs, k_new, v_new = _in_proj_conv(xp, w_in_b, l, zero_ctx, cw, cp, seq_rows=t_p, keep=keep)
        am = _attn_prompt(h, mk_pb, mv_pb, l, bias, n_seq=n_p, t_len=t_p, tq=512, col0=-N_GLU_BLOCKS)
        xp = _out_proj_conv(yc, h, am, xp, cp, w_pw_b, w_out_b, l, gb, alpha=alpha)
        conv_p.append(cs)
        kp_l.append(k_new)
        vp_l.append(v_new)

        h, k_new, v_new = _in_proj(xs, w_in_b, l, seq_rows=n_s * t_s, keep=n_s * t_s)
        cm, cs = _conv_branch(h, cache_ctx[l], cw, cp, w_pw_b, l, n_seq=n_s, t_len=t_s, tq=t_s)
        am = _attn_sample(h, kc_b, vc_b, mk_sb, mv_sb, l, bias[-1, :, :CHUNK, :BAND], n_seq=n_s)
        xs = _out_proj(cm, am, xs, w_out_b, l, gb, alpha=alpha)
        conv_s.append(cs)
        ks_l.append(k_new)
        vs_l.append(v_new)

    mem_shape = (depth, n_p, n_mem, N_HEADS, HEAD_DIM)
    kv_p_shape = (depth, n_p, keep, N_HEADS, HEAD_DIM)
    kv_s_shape = (depth, n_s, t_s, N_HEADS, HEAD_DIM)
    return (xp.reshape(n_p, t_p, D_MODEL), xs.reshape(n_s, t_s, D_MODEL), jnp.stack(conv_p),
            jnp.stack(kp_l).reshape(kv_p_shape), jnp.stack(vp_l).reshape(kv_p_shape),
            mk_p.reshape(mem_shape), mv_p.reshape(mem_shape),
            jnp.stack(conv_s), jnp.stack(ks_l).reshape(kv_s_shape), jnp.stack(vs_l).reshape(kv_s_shape))
```

```python
import functools

import jax
import jax.numpy as jnp
from jax import lax
from jax.experimental import pallas as pl
from jax.experimental.pallas import tpu as pltpu

D_MODEL = 2048
CHUNK = 64
N_LEFT_CHUNKS = 8
ATT_WIN = N_LEFT_CHUNKS * CHUNK
BAND = ATT_WIN + CHUNK
CONV_W = 31
CONV_LEFT = CONV_W - 1
D_CONV = D_MODEL // 2
HEAD_DIM = 128
N_HEADS = 4
D_ATT = N_HEADS * HEAD_DIM
N_MEM = 256
REL_CLIP = 128
D_MIX = D_CONV + 2 * D_ATT
D_IN = 3 * D_CONV + 6 * D_ATT
LN_EPS = 1e-5
SM_SCALE = HEAD_DIM ** -0.5
MASK_VALUE = -1e30

COL_Q, COL_K, COL_V, COL_AG, COL_MQ, COL_MG = 6, 7, 8, 9, 10, 11
N_COL_BLOCKS = D_IN // D_ATT
N_GLU_BLOCKS = 4

LANES = 128
SUBLANES = 8
HALO = 32

Q_CHUNKS = 2
Q_TILE = Q_CHUNKS * CHUNK
Q_BAND = ATT_WIN + Q_TILE
N_BIAS_VARIANTS = N_LEFT_CHUNKS // Q_CHUNKS + 1

BF16 = jnp.bfloat16
F32 = jnp.float32

VMEM_LIMIT_BYTES = 56 * 1024 * 1024
ONCE = pl.Buffered(1)


def _compiler_params(n_axes):
    return pltpu.CompilerParams(
        dimension_semantics=("arbitrary",) * n_axes, vmem_limit_bytes=VMEM_LIMIT_BYTES)


def _sigmoid(x):
    return 1.0 / (1.0 + jnp.exp(-x))


def _layer_norm(z, g, b):
    mu = jnp.mean(z, axis=-1, keepdims=True)
    d = z - mu
    var = jnp.mean(d * d, axis=-1, keepdims=True)
    return d * lax.rsqrt(var + LN_EPS) * g + b


IN_PROJ_ROWS = 256


def _store_tail(tail_ref, acc, r0, tm):
    keep = tail_ref.shape[0]
    n = acc.shape[0]
    if r0 + n > tm - keep:
        lo = max(r0, tm - keep)
        tail_ref[lo - (tm - keep):r0 + n - (tm - keep), :] = acc[lo - r0:, :]


def _in_proj_kernel(x_ref, w_ref, h_ref, k_ref, v_ref, xb_ref, tail_ref):
    j = pl.program_id(1)
    tm = x_ref.shape[0]

    @pl.when(j == 0)
    def _():
        xb_ref[...] = x_ref[...].astype(BF16)

    is_sig = (j == 2) | (j == 3)
    is_silu = (j == 4) | (j == 5) | (j == COL_AG) | (j == COL_MG)
    n_rows = min(IN_PROJ_ROWS, tm)
    for r0 in range(0, tm, n_rows):
        rows = slice(r0, r0 + n_rows)
        acc = jnp.dot(xb_ref[rows, :], w_ref[0], preferred_element_type=F32)
        sg = _sigmoid(acc)
        out = jnp.where(is_sig, sg, jnp.where(is_silu, acc * sg, acc))
        h_ref[rows, :] = out.astype(BF16)
        _store_tail(tail_ref, acc, r0, tm)

    @pl.when(j == COL_K)
    def _():
        k_ref[...] = tail_ref[...]

    @pl.when(j == COL_V)
    def _():
        v_ref[...] = tail_ref[...]


def _in_proj(x, w, layer, *, seq_rows, keep):
    m = x.shape[0]
    tm = min(1024, m)
    assert m % tm == 0 and seq_rows % tm == 0 and keep <= tm and tm % min(IN_PROJ_ROWS, tm) == 0
    tiles_per_seq = seq_rows // tm
    n_seq = m // seq_rows
    keep_spec = pl.BlockSpec((keep, D_ATT), lambda i, j: (i // tiles_per_seq, 0))
    return pl.pallas_call(
        _in_proj_kernel,
        grid=(m // tm, N_COL_BLOCKS),
        in_specs=[
            pl.BlockSpec((tm, D_MODEL), lambda i, j: (i, 0)),
            pl.BlockSpec((1, D_MODEL, D_ATT), lambda i, j: (layer, 0, j)),
        ],
        out_specs=[pl.BlockSpec((tm, D_ATT), lambda i, j: (i, j)), keep_spec, keep_spec],
        out_shape=[
            jax.ShapeDtypeStruct((m, D_IN), BF16),
            jax.ShapeDtypeStruct((n_seq * keep, D_ATT), F32),
            jax.ShapeDtypeStruct((n_seq * keep, D_ATT), F32),
        ],
        scratch_shapes=[pltpu.VMEM((tm, D_MODEL), BF16), pltpu.VMEM((keep, D_ATT), F32)],
        compiler_params=_compiler_params(2),
        name="in_proj",
    )(x, w)


TAP_ROWS = 32


def _conv_offsets(s):
    return [o for o in range(HALO - CONV_LEFT, HALO + 1) if o % SUBLANES == s]


def _in_proj_conv_kernel(x_ref, w_ref, ctx_ref, cw_ref, cb_ref, h_ref, y_ref, st_ref, k_ref, v_ref,
                         xb_ref, tail_ref, up_ref, *, tiles_per_seq):
    i = pl.program_id(0)
    j = pl.program_id(1)
    tm = x_ref.shape[0]
    n_rows = min(IN_PROJ_ROWS, tm)
    seq_start = i % tiles_per_seq == 0

    @pl.when(j == 0)
    def _():
        xb_ref[...] = x_ref[...].astype(BF16)

    @pl.when((j == 0) & seq_start)
    def _():
        up_ref[0:HALO, :] = ctx_ref[0]

    @pl.when((j == 0) & jnp.logical_not(seq_start))
    def _():
        up_ref[0:HALO, :] = up_ref[tm:tm + HALO, :]

    def matmul_blocks(epilogue):
        for r0 in range(0, tm, n_rows):
            acc = jnp.dot(xb_ref[r0:r0 + n_rows, :], w_ref[0], preferred_element_type=F32)
            epilogue(r0, acc)

    for jj in range(N_GLU_BLOCKS):
        cols = slice((jj % 2) * D_ATT, (jj % 2 + 1) * D_ATT)

        @pl.when(j == jj)
        def _(jj=jj, cols=cols):
            def glu(r0, acc):
                rows = slice(HALO + r0, HALO + r0 + n_rows)
                if jj < 2:
                    up_ref[rows, cols] = acc
                else:
                    up_ref[rows, cols] = up_ref[rows, cols] * _sigmoid(acc)
            matmul_blocks(glu)

    @pl.when(j == N_GLU_BLOCKS)
    def _():
        st_ref[0] = up_ref[HALO + tm - CONV_LEFT:HALO + tm, :]

    @pl.when(j >= N_GLU_BLOCKS)
    def _():
        is_silu = (j == 4) | (j == 5) | (j == COL_AG) | (j == COL_MG)

        lane = pl.ds(pl.multiple_of((j - N_GLU_BLOCKS) * LANES, LANES), LANES)
        groups = TAP_ROWS // SUBLANES

        def conv_rows(r0):
            base = up_ref[r0:r0 + HALO + TAP_ROWS, lane]
            parts = [None, None]
            for s in range(SUBLANES):
                offs = _conv_offsets(s)
                lo = offs[0] - s
                win = base[s + lo:offs[-1] + TAP_ROWS]
                for o in offs:
                    tap = win[o - s - lo:o - s - lo + TAP_ROWS]
                    w = cw_ref[o - (HALO - CONV_LEFT), :, lane]
                    term = tap.reshape(groups, SUBLANES, LANES) * w[None]
                    parts[s % 2] = term if parts[s % 2] is None else parts[s % 2] + term
            acc = (parts[0] + parts[1]).reshape(TAP_ROWS, LANES)
            y_ref[r0:r0 + TAP_ROWS, lane] = acc + cb_ref[0:1, lane]

        n_mm = tm // n_rows
        n_tap_blocks = tm // TAP_ROWS
        weights = [3] * (n_mm - 1) + [1]
        bounds = [n_tap_blocks * sum(weights[:r]) // sum(weights) for r in range(n_mm)] + [n_tap_blocks]
        for r in range(n_mm):
            r0 = r * n_rows
            acc = jnp.dot(xb_ref[r0:r0 + n_rows, :], w_ref[0], preferred_element_type=F32)
            for blk in range(bounds[r], bounds[r + 1]):
                conv_rows(blk * TAP_ROWS)
            out = jnp.where(is_silu, acc * _sigmoid(acc), acc)
            h_ref[r0:r0 + n_rows, :] = out.astype(BF16)
            _store_tail(tail_ref, acc, r0, tm)

    @pl.when(j == COL_K)
    def _():
        k_ref[...] = tail_ref[...]

    @pl.when(j == COL_V)
    def _():
        v_ref[...] = tail_ref[...]


def _in_proj_conv(x, w, layer, ctx, cw, cb, *, seq_rows, keep):
    m = x.shape[0]
    tm = min(1024, m)
    assert m % tm == 0 and seq_rows % tm == 0 and keep <= tm and tm % min(IN_PROJ_ROWS, tm) == 0
    assert tm % TAP_ROWS == 0 and N_COL_BLOCKS - N_GLU_BLOCKS == D_CONV // LANES
    tiles_per_seq = seq_rows // tm
    n_seq = m // seq_rows
    seq_of = lambda i, j: i // tiles_per_seq
    keep_spec = pl.BlockSpec((keep, D_ATT), lambda i, j: (seq_of(i, j), 0))
    return pl.pallas_call(
        functools.partial(_in_proj_conv_kernel, tiles_per_seq=tiles_per_seq),
        grid=(m // tm, N_COL_BLOCKS),
        in_specs=[
            pl.BlockSpec((tm, D_MODEL), lambda i, j: (i, 0)),
            pl.BlockSpec((1, D_MODEL, D_ATT), lambda i, j: (layer, 0, j)),
            pl.BlockSpec((1, HALO, D_CONV), lambda i, j: (seq_of(i, j), 0, 0)),
            pl.BlockSpec((CONV_W, SUBLANES, D_CONV), lambda i, j: (0, 0, 0), pipeline_mode=ONCE),
            pl.BlockSpec((8, D_CONV), lambda i, j: (0, 0), pipeline_mode=ONCE),
        ],
        out_specs=[
            pl.BlockSpec((tm, D_ATT), lambda i, j: (i, jnp.maximum(j - N_GLU_BLOCKS, 0))),
            pl.BlockSpec((tm, D_CONV), lambda i, j: (i, 0)),
            pl.BlockSpec((1, CONV_LEFT, D_CONV), lambda i, j: (seq_of(i, j), 0, 0)),
            keep_spec, keep_spec,
        ],
        out_shape=[
            jax.ShapeDtypeStruct((m, D_IN - N_GLU_BLOCKS * D_ATT), BF16),
            jax.ShapeDtypeStruct((m, D_CONV), F32),
            jax.ShapeDtypeStruct((n_seq, CONV_LEFT, D_CONV), F32),
            jax.ShapeDtypeStruct((n_seq * keep, D_ATT), F32),
            jax.ShapeDtypeStruct((n_seq * keep, D_ATT), F32),
        ],
        scratch_shapes=[
            pltpu.VMEM((tm, D_MODEL), BF16),
            pltpu.VMEM((keep, D_ATT), F32),
            pltpu.VMEM((HALO + tm, D_CONV), F32),
        ],
        compiler_params=pltpu.CompilerParams(
            dimension_semantics=("arbitrary", "arbitrary"), vmem_limit_bytes=60 * 1024 * 1024),
        name="in_proj_conv",
    )(x, w, ctx, cw, cb)


def _mem_kv_kernel(x_ref, w_ref, k_ref, v_ref, kb_ref, vb_ref):
    acc = jnp.dot(x_ref[...], w_ref[0], preferred_element_type=F32)
    k = acc[:, :D_ATT]
    v = acc[:, D_ATT:]
    k_ref[0] = k
    v_ref[0] = v
    kb_ref[0] = k.astype(BF16)
    vb_ref[0] = v.astype(BF16)


def _mem_kv(mem, w):
    m = mem.shape[0]
    depth = w.shape[0]
    tm = min(1024, m)
    assert m % tm == 0
    out_spec = pl.BlockSpec((1, tm, D_ATT), lambda i, l: (l, i, 0))
    return pl.pallas_call(
        _mem_kv_kernel,
        grid=(m // tm, depth),
        in_specs=[
            pl.BlockSpec((tm, D_MODEL), lambda i, l: (i, 0)),
            pl.BlockSpec((1, D_MODEL, 2 * D_ATT), lambda i, l: (l, 0, 0)),
        ],
        out_specs=[out_spec] * 4,
        out_shape=[jax.ShapeDtypeStruct((depth, m, D_ATT), F32)] * 2
        + [jax.ShapeDtypeStruct((depth, m, D_ATT), BF16)] * 2,
        compiler_params=_compiler_params(2),
        name="mem_kv",
    )(mem, w)


CONV_ROWS = 32
CONV_LANES = 256


def _conv_kernel(a_ref, sb_ref, g_ref, ah_ref, sbh_ref, ctx_ref, cw_ref, cp_ref, wpw_ref,
                 o_ref, st_ref, up_ref, us_ref, y_ref, *, tq, n_t):
    t = pl.program_id(1)

    up_ref[HALO:HALO + tq, :] = a_ref[...].astype(F32) * sb_ref[...].astype(F32)

    @pl.when(t == 0)
    def _():
        up_ref[0:HALO, :] = ctx_ref[0]

    @pl.when(t > 0)
    def _():
        up_ref[0:HALO, :] = ah_ref[...].astype(F32) * sbh_ref[...].astype(F32)

    n_shifted = tq + HALO - SUBLANES
    for s in range(1, SUBLANES):
        us_ref[s - 1, 0:n_shifted, :] = up_ref[s:s + n_shifted, :]

    groups = CONV_ROWS // SUBLANES
    for lc in range(D_CONV // CONV_LANES):
        lanes = slice(lc * CONV_LANES, (lc + 1) * CONV_LANES)

        def row_block(rb, carry, lanes=lanes):
            r0 = pl.multiple_of(rb * CONV_ROWS, CONV_ROWS)
            parts = [None, None]
            for s in range(SUBLANES):
                offs = _conv_offsets(s)
                lo = offs[0] - s
                rows = pl.ds(r0 + lo, offs[-1] - s + CONV_ROWS - lo)
                win = up_ref[rows, lanes] if s == 0 else us_ref[s - 1, rows, lanes]
                for o in offs:
                    tap = win[o - s - lo:o - s - lo + CONV_ROWS]
                    w = cw_ref[o - (HALO - CONV_LEFT), :, lanes]
                    term = tap.reshape(groups, SUBLANES, CONV_LANES) * w[None]
                    parts[s % 2] = term if parts[s % 2] is None else parts[s % 2] + term
            acc = parts[0] + parts[1]
            y_ref[pl.ds(r0, CONV_ROWS), lanes] = acc.reshape(CONV_ROWS, CONV_LANES) + cp_ref[0:1, lanes]
            return carry

        lax.fori_loop(0, tq // CONV_ROWS, row_block, 0)

    yn = _layer_norm(y_ref[...], cp_ref[1:2, :], cp_ref[2:3, :])
    sw = yn * _sigmoid(yn)
    pw = jnp.dot(sw.astype(BF16), wpw_ref[0], preferred_element_type=F32)
    o_ref[...] = (pw * g_ref[...].astype(F32)).astype(BF16)

    @pl.when(t == n_t - 1)
    def _():
        st_ref[0] = up_ref[HALO + tq - CONV_LEFT:HALO + tq, :]


def _conv_branch(h, ctx, cw, cp, wpw, layer, *, n_seq, t_len, tq):
    assert t_len % tq == 0 and tq % HALO == 0 and tq >= HALO
    n_t = t_len // tq
    m = n_seq * t_len

    def row_map(col):
        return lambda b, t: (b * n_t + t, col)

    def halo_map(col):
        return lambda b, t: (jnp.maximum((b * t_len + t * tq) // HALO - 1, 0), col)

    return pl.pallas_call(
        functools.partial(_conv_kernel, tq=tq, n_t=n_t),
        grid=(n_seq, n_t),
        in_specs=[
            pl.BlockSpec((tq, D_CONV), row_map(0)),
            pl.BlockSpec((tq, D_CONV), row_map(1)),
            pl.BlockSpec((tq, D_CONV), row_map(2)),
            pl.BlockSpec((HALO, D_CONV), halo_map(0)),
            pl.BlockSpec((HALO, D_CONV), halo_map(1)),
            pl.BlockSpec((1, HALO, D_CONV), lambda b, t: (b, 0, 0)),
            pl.BlockSpec((CONV_W, SUBLANES, D_CONV), lambda b, t: (0, 0, 0)),
            pl.BlockSpec((8, D_CONV), lambda b, t: (0, 0)),
            pl.BlockSpec((1, D_CONV, D_CONV), lambda b, t: (layer, 0, 0)),
        ],
        out_specs=[
            pl.BlockSpec((tq, D_CONV), lambda b, t: (b * n_t + t, 0)),
            pl.BlockSpec((1, CONV_LEFT, D_CONV), lambda b, t: (b, 0, 0)),
        ],
        out_shape=[
            jax.ShapeDtypeStruct((m, D_CONV), BF16),
            jax.ShapeDtypeStruct((n_seq, CONV_LEFT, D_CONV), F32),
        ],
        scratch_shapes=[
            pltpu.VMEM((HALO + tq, D_CONV), F32),
            pltpu.VMEM((SUBLANES - 1, HALO + tq, D_CONV), F32),
            pltpu.VMEM((tq, D_CONV), F32),
        ],
        compiler_params=_compiler_params(2),
        name="conv_branch",
    )(h, h, h, h, h, ctx, cw, cp, wpw)


def _qk(q, k):
    return lax.dot_general(q, k, (((1,), (1,)), ((), ())), preferred_element_type=F32)


def _attend(qs, ks, vs, biases):
    ss = [_qk(q, k) * SM_SCALE for q, k in zip(qs, ks)]
    ss = [s if b is None else s + b for s, b in zip(ss, biases)]
    ps, ls = [], []
    for s in ss:
        m = jnp.max(s, axis=-1, keepdims=True)
        p = jnp.exp(s - m)
        ls.append(jnp.sum(p, axis=-1, keepdims=True))
        ps.append(p.astype(BF16))
    return [jnp.dot(p, v, preferred_element_type=F32) / l for p, v, l in zip(ps, vs, ls)]


def _head_cols(h):
    return slice(h * HEAD_DIM, (h + 1) * HEAD_DIM)


def _memory_attention(mq_ref, mk_ref, mv_ref, mg_ref, o_ref):
    heads = range(N_HEADS)
    os = _attend([mq_ref[:, _head_cols(h)] for h in heads], [mk_ref[0, 0, :, _head_cols(h)] for h in heads],
                 [mv_ref[0, 0, :, _head_cols(h)] for h in heads], [None] * N_HEADS)
    for h in heads:
        o_ref[:, D_ATT + h * HEAD_DIM:D_ATT + (h + 1) * HEAD_DIM] = (
            os[h] * mg_ref[:, _head_cols(h)].astype(F32)).astype(BF16)


def _attn_prompt_kernel(q_ref, k_ref, v_ref, ag_ref, mq_ref, mg_ref, mk_ref, mv_ref, bias_ref,
                        o_ref, *, tq):
    t = pl.program_id(1)
    n_q = tq // Q_TILE
    heads = range(N_HEADS)

    def q_tile(qi, carry):
        c0 = (t * n_q + qi) * Q_CHUNKS
        start = pl.multiple_of(jnp.maximum(c0 * CHUNK - ATT_WIN, 0), Q_TILE)
        variant = jnp.minimum(c0 // Q_CHUNKS, N_BIAS_VARIANTS - 1)
        rows = pl.ds(pl.multiple_of(qi * Q_TILE, Q_TILE), Q_TILE)
        band = pl.ds(start, Q_BAND)
        os = _attend([q_ref[rows, _head_cols(h)] for h in heads], [k_ref[band, _head_cols(h)] for h in heads],
                     [v_ref[band, _head_cols(h)] for h in heads], [bias_ref[variant, h] for h in heads])
        for h in heads:
            o_ref[rows, _head_cols(h)] = (os[h] * ag_ref[rows, _head_cols(h)].astype(F32)).astype(BF16)
        return carry

    lax.fori_loop(0, n_q, q_tile, 0)
    _memory_attention(mq_ref, mk_ref, mv_ref, mg_ref, o_ref)


def _attn_prompt(h, mk, mv, layer, bias, *, n_seq, t_len, tq, col0):
    assert t_len % tq == 0 and tq % Q_TILE == 0 and t_len >= Q_BAND
    n_t = t_len // tq

    def row_map(col):
        return lambda b, t: (b * n_t + t, col0 + col)

    return pl.pallas_call(
        functools.partial(_attn_prompt_kernel, tq=tq),
        grid=(n_seq, n_t),
        in_specs=[
            pl.BlockSpec((tq, D_ATT), row_map(COL_Q)),
            pl.BlockSpec((t_len, D_ATT), lambda b, t: (b, col0 + COL_K)),
            pl.BlockSpec((t_len, D_ATT), lambda b, t: (b, col0 + COL_V)),
            pl.BlockSpec((tq, D_ATT), row_map(COL_AG)),
            pl.BlockSpec((tq, D_ATT), row_map(COL_MQ)),
            pl.BlockSpec((tq, D_ATT), row_map(COL_MG)),
            pl.BlockSpec((1, 1, N_MEM, D_ATT), lambda b, t: (layer, b, 0, 0)),
            pl.BlockSpec((1, 1, N_MEM, D_ATT), lambda b, t: (layer, b, 0, 0)),
            pl.BlockSpec(bias.shape, lambda b, t: (0, 0, 0, 0)),
        ],
        out_specs=pl.BlockSpec((tq, 2 * D_ATT), lambda b, t: (b * n_t + t, 0)),
        out_shape=jax.ShapeDtypeStruct((n_seq * t_len, 2 * D_ATT), BF16),
        compiler_params=_compiler_params(2),
        name="attn_prompt",
    )(h, h, h, h, h, h, mk, mv, bias)


def _attn_sample_kernel(q_ref, k_ref, v_ref, ag_ref, mq_ref, mg_ref, kc_ref, vc_ref, mk_ref, mv_ref,
                        bias_ref, o_ref):
    heads = range(N_HEADS)
    ks = [jnp.concatenate([kc_ref[0, 0, :, _head_cols(h)], k_ref[:, _head_cols(h)]], axis=0) for h in heads]
    vs = [jnp.concatenate([vc_ref[0, 0, :, _head_cols(h)], v_ref[:, _head_cols(h)]], axis=0) for h in heads]
    os = _attend([q_ref[:, _head_cols(h)] for h in heads], ks, vs, [bias_ref[h] for h in heads])
    for h in heads:
        o_ref[:, _head_cols(h)] = (os[h] * ag_ref[:, _head_cols(h)].astype(F32)).astype(BF16)
    _memory_attention(mq_ref, mk_ref, mv_ref, mg_ref, o_ref)


def _attn_sample(h, kc, vc, mk, mv, layer, bias, *, n_seq):
    def row_map(col):
        return lambda b: (b, col)

    seq_spec = lambda n: pl.BlockSpec((1, 1, n, D_ATT), lambda b: (layer, b, 0, 0))
    return pl.pallas_call(
        _attn_sample_kernel,
        grid=(n_seq,),
        in_specs=[
            pl.BlockSpec((CHUNK, D_ATT), row_map(COL_Q)),
            pl.BlockSpec((CHUNK, D_ATT), row_map(COL_K)),
            pl.BlockSpec((CHUNK, D_ATT), row_map(COL_V)),
            pl.BlockSpec((CHUNK, D_ATT), row_map(COL_AG)),
            pl.BlockSpec((CHUNK, D_ATT), row_map(COL_MQ)),
            pl.BlockSpec((CHUNK, D_ATT), row_map(COL_MG)),
            seq_spec(ATT_WIN), seq_spec(ATT_WIN), seq_spec(N_MEM), seq_spec(N_MEM),
            pl.BlockSpec(bias.shape, lambda b: (0, 0, 0)),
        ],
        out_specs=pl.BlockSpec((CHUNK, 2 * D_ATT), lambda b: (b, 0)),
        out_shape=jax.ShapeDtypeStruct((n_seq * CHUNK, 2 * D_ATT), BF16),
        compiler_params=_compiler_params(1),
        name="attn_sample",
    )(h, h, h, h, h, h, kc, vc, mk, mv, bias)


OUT_COLS = 512


def _out_proj_kernel(cm_ref, am_ref, x_ref, w_ref, gb_ref, y_ref, z_ref, *, alpha):
    tm = x_ref.shape[0]
    row_sum = jnp.zeros((tm, 1), F32)
    for n in range(D_MODEL // OUT_COLS):
        cols = slice(n * OUT_COLS, (n + 1) * OUT_COLS)
        z = jnp.dot(cm_ref[...], w_ref[0, 0:D_CONV, cols], preferred_element_type=F32)
        z = z + jnp.dot(am_ref[...], w_ref[0, D_CONV:D_MIX, cols], preferred_element_type=F32)
        z = alpha * x_ref[:, cols] + z
        z_ref[:, cols] = z
        row_sum = row_sum + jnp.sum(z, axis=-1, keepdims=True)
    mu = row_sum * (1.0 / D_MODEL)
    sq_sum = jnp.zeros((tm, 1), F32)
    for n in range(D_MODEL // OUT_COLS):
        cols = slice(n * OUT_COLS, (n + 1) * OUT_COLS)
        d = z_ref[:, cols] - mu
        sq_sum = sq_sum + jnp.sum(d * d, axis=-1, keepdims=True)
    inv = lax.rsqrt(sq_sum * (1.0 / D_MODEL) + LN_EPS)
    for n in range(D_MODEL // OUT_COLS):
        cols = slice(n * OUT_COLS, (n + 1) * OUT_COLS)
        y_ref[:, cols] = (z_ref[:, cols] - mu) * inv * gb_ref[0:1, cols] + gb_ref[1:2, cols]


def _out_proj(cm, am, x, w, layer, gb, *, alpha):
    m = x.shape[0]
    tm = min(512, m)
    assert m % tm == 0
    return pl.pallas_call(
        functools.partial(_out_proj_kernel, alpha=alpha),
        grid=(m // tm,),
        in_specs=[
            pl.BlockSpec((tm, D_CONV), lambda i: (i, 0)),
            pl.BlockSpec((tm, 2 * D_ATT), lambda i: (i, 0)),
            pl.BlockSpec((tm, D_MODEL), lambda i: (i, 0)),
            pl.BlockSpec((1, D_MIX, D_MODEL), lambda i: (layer, 0, 0)),
            pl.BlockSpec((8, D_MODEL), lambda i: (0, 0)),
        ],
        out_specs=pl.BlockSpec((tm, D_MODEL), lambda i: (i, 0)),
        out_shape=jax.ShapeDtypeStruct((m, D_MODEL), F32),
        scratch_shapes=[pltpu.VMEM((tm, D_MODEL), F32)],
        compiler_params=_compiler_params(1),
        name="out_proj",
    )(cm, am, x, w, gb)


OUT_ROWS = 256


def _out_proj_conv_kernel(yc_ref, g_ref, am_ref, x_ref, cp_ref, wpw_ref, w_ref, gb_ref, y_ref, *, alpha):
    tm = x_ref.shape[0]
    n_rows = min(OUT_ROWS, tm)
    for r0 in range(0, tm, n_rows):
        rows = slice(r0, r0 + n_rows)
        yn = _layer_norm(yc_ref[rows, :], cp_ref[1:2, :], cp_ref[2:3, :])
        sw = yn * _sigmoid(yn)
        pw = jnp.dot(sw.astype(BF16), wpw_ref[0], preferred_element_type=F32)
        cm = (pw * g_ref[rows, :].astype(F32)).astype(BF16)
        z = jnp.dot(cm, w_ref[0, 0:D_CONV, :], preferred_element_type=F32)
        z = z + jnp.dot(am_ref[rows, :], w_ref[0, D_CONV:D_MIX, :], preferred_element_type=F32)
        z = z + alpha * x_ref[rows, :]
        y_ref[rows, :] = _layer_norm(z, gb_ref[0:1, :], gb_ref[1:2, :])


def _out_proj_conv(yc, h, am, x, cp, wpw, w, layer, gb, *, alpha):
    m = x.shape[0]
    tm = min(512, m)
    assert m % tm == 0 and tm % min(OUT_ROWS, tm) == 0
    rows_of = lambda width: pl.BlockSpec((tm, width), lambda i: (i, 0))
    whole = lambda shape: pl.BlockSpec(shape, lambda i: (0, 0), pipeline_mode=ONCE)
    of_layer = lambda shape: pl.BlockSpec((1,) + shape, lambda i: (layer, 0, 0), pipeline_mode=ONCE)
    return pl.pallas_call(
        functools.partial(_out_proj_conv_kernel, alpha=alpha),
        grid=(m // tm,),
        in_specs=[
            rows_of(D_CONV), rows_of(D_CONV), rows_of(2 * D_ATT), rows_of(D_MODEL),
            whole((8, D_CONV)), of_layer((D_CONV, D_CONV)), of_layer((D_MIX, D_MODEL)), whole((8, D_MODEL)),
        ],
        out_specs=rows_of(D_MODEL),
        out_shape=jax.ShapeDtypeStruct((m, D_MODEL), F32),
        compiler_params=_compiler_params(1),
        name="out_proj_conv",
    )(yc, h, am, x, cp, wpw, w, gb)


def _band_bias(table):
    n_h = table.shape[0]
    n_u = Q_TILE + Q_BAND - 1
    n_flat = ATT_WIN + Q_TILE - REL_CLIP
    lo = REL_CLIP - (Q_TILE - 1)
    assert lo >= 0 and n_flat + 2 * REL_CLIP - lo == n_u
    u = jnp.concatenate([jnp.broadcast_to(table[:, 2 * REL_CLIP:], (n_h, n_flat)),
                         jnp.flip(table[:, lo:2 * REL_CLIP], axis=1)], axis=1).astype(F32)
    u = jnp.pad(u, ((0, 0), (0, 1)))
    skew = jnp.broadcast_to(u[:, None, :], (n_h, Q_TILE, n_u + 1)).reshape(n_h, Q_TILE * (n_u + 1))
    full = skew[:, Q_TILE - 1:Q_TILE - 1 + Q_TILE * n_u].reshape(n_h, Q_TILE, n_u)[:, :, :Q_BAND]
    qc = jnp.arange(Q_TILE)[:, None] // CHUNK
    kc = jnp.arange(Q_BAND)[None, :] // CHUNK
    full = jnp.where((kc >= qc) & (kc <= qc + N_LEFT_CHUNKS), full, MASK_VALUE)
    variants = []
    for v in range(N_BIAS_VARIANTS - 1):
        off = ATT_WIN - v * Q_TILE
        pad = jnp.full((n_h, Q_TILE, off), MASK_VALUE, F32)
        variants.append(jnp.concatenate([full[:, :, off:], pad], axis=-1))
    variants.append(full)
    return jnp.stack(variants)


def _pack_rows(rows, width):
    return jnp.concatenate([jnp.stack(rows).astype(F32), jnp.zeros((8 - len(rows), width), F32)])


def kernel(x_prompt, x_sample, mem_prompt, cache_conv, cache_att_k, cache_att_v, cache_mem_k, cache_mem_v,
           w_in, conv_w, conv_b, conv_ln_g, conv_ln_b, w_pw, rel_table, w_mem_kv, w_out, ln_g, ln_b):
    n_p, t_p, _ = x_prompt.shape
    n_s, t_s, _ = x_sample.shape
    depth = w_in.shape[0]
    n_mem = mem_prompt.shape[1]
    assert t_s == CHUNK and n_mem == N_MEM and cache_att_k.shape[2] == ATT_WIN
    alpha = (2 * depth) ** 0.25
    keep = min(ATT_WIN, t_p)

    w_in_b = w_in.astype(BF16)
    w_pw_b = w_pw.astype(BF16)
    w_out_b = w_out.astype(BF16)
    w_mkv_b = w_mem_kv.astype(BF16)

    mk_p, mv_p, mk_pb, mv_pb = _mem_kv(mem_prompt.reshape(n_p * n_mem, D_MODEL).astype(BF16), w_mkv_b)
    mk_pb = mk_pb.reshape(depth, n_p, n_mem, D_ATT)
    mv_pb = mv_pb.reshape(depth, n_p, n_mem, D_ATT)

    xp = x_prompt.reshape(n_p * t_p, D_MODEL)
    xs = x_sample.reshape(n_s * t_s, D_MODEL)
    zero_ctx = jnp.zeros((n_p, HALO, D_CONV), F32)
    cache_ctx = jnp.pad(cache_conv, ((0, 0), (0, 0), (HALO - CONV_LEFT, 0), (0, 0)))
    kc_b = cache_att_k.reshape(depth, n_s, ATT_WIN, D_ATT).astype(BF16)
    vc_b = cache_att_v.reshape(depth, n_s, ATT_WIN, D_ATT).astype(BF16)
    mk_sb = cache_mem_k.reshape(depth, n_s, n_mem, D_ATT).astype(BF16)
    mv_sb = cache_mem_v.reshape(depth, n_s, n_mem, D_ATT).astype(BF16)

    conv_p, kp_l, vp_l, conv_s, ks_l, vs_l = [], [], [], [], [], []
    for l in range(depth):
        cw = jnp.broadcast_to(conv_w[l][:, None, :], (CONV_W, SUBLANES, D_CONV))
        cp = _pack_rows([conv_b[l], conv_ln_g[l], conv_ln_b[l]], D_CONV)
        gb = _pack_rows([ln_g[l], ln_b[l]], D_MODEL)
        bias = _band_bias(rel_table[l])

        h, yc, cs, k_new, v_new = _in_proj_conv(xp, w_in_b, l, zero_ctx, cw, cp, seq_rows=t_p, keep=keep)
        am = _attn_prompt(h, mk_pb, mv_pb, l, bias, n_seq=n_p, t_len=t_p, tq=512, col0=-N_GLU_BLOCKS)
        xp = _out_proj_conv(yc, h, am, xp, cp, w_pw_b, w_out_b, l, gb, alpha=alpha)
        conv_p.append(cs)
        kp_l.append(k_new)
        vp_l.append(v_new)

        h, k_new, v_new = _in_proj(xs, w_in_b, l, seq_rows=n_s * t_s, keep=n_s * t_s)
        cm, cs = _conv_branch(h, cache_ctx[l], cw, cp, w_pw_b, l, n_seq=n_s, t_len=t_s, tq=t_s)
        am = _attn_sample(h, kc_b, vc_b, mk_sb, mv_sb, l, bias[-1, :, :CHUNK, :BAND], n_seq=n_s)
        xs = _out_proj(cm, am, xs, w_out_b, l, gb, alpha=alpha)
        conv_s.append(cs)
        ks_l.append(k_new)
        vs_l.append(v_new)

    mem_shape = (depth, n_p, n_mem, N_HEADS, HEAD_DIM)
    kv_p_shape = (depth, n_p, keep, N_HEADS, HEAD_DIM)
    kv_s_shape = (depth, n_s, t_s, N_HEADS, HEAD_DIM)
    return (xp.reshape(n_p, t_p, D_MODEL), xs.reshape(n_s, t_s, D_MODEL), jnp.stack(conv_p),
            jnp.stack(kp_l).reshape(kv_p_shape), jnp.stack(vp_l).reshape(kv_p_shape),
            mk_p.reshape(mem_shape), mv_p.reshape(mem_shape),
            jnp.stack(conv_s), jnp.stack(ks_l).reshape(kv_s_shape), jnp.stack(vs_l).reshape(kv_s_shape))
```

```python
import functools

import jax
import jax.numpy as jnp
from jax import lax
from jax.experimental import pallas as pl
from jax.experimental.pallas import tpu as pltpu

D_MODEL = 2048
CHUNK = 64
N_LEFT_CHUNKS = 8
ATT_WIN = N_LEFT_CHUNKS * CHUNK
BAND = ATT_WIN + CHUNK
CONV_W = 31
CONV_LEFT = CONV_W - 1
D_CONV = D_MODEL // 2
HEAD_DIM = 128
N_HEADS = 4
D_ATT = N_HEADS * HEAD_DIM
N_MEM = 256
REL_CLIP = 128
D_MIX = D_CONV + 2 * D_ATT
D_IN = 3 * D_CONV + 6 * D_ATT
LN_EPS = 1e-5
SM_SCALE = HEAD_DIM ** -0.5
MASK_VALUE = -1e30

COL_Q, COL_K, COL_V, COL_AG, COL_MQ, COL_MG = 6, 7, 8, 9, 10, 11
N_COL_BLOCKS = D_IN // D_ATT
N_GLU_BLOCKS = 4

LANES = 128
SUBLANES = 8
HALO = 32

Q_CHUNKS = 2
Q_TILE = Q_CHUNKS * CHUNK
Q_BAND = ATT_WIN + Q_TILE
N_BIAS_VARIANTS = N_LEFT_CHUNKS // Q_CHUNKS + 1

BF16 = jnp.bfloat16
F32 = jnp.float32

VMEM_LIMIT_BYTES = 56 * 1024 * 1024
ONCE = pl.Buffered(1)


def _compiler_params(n_axes):
    return pltpu.CompilerParams(
        dimension_semantics=("arbitrary",) * n_axes, vmem_limit_bytes=VMEM_LIMIT_BYTES)


def _sigmoid(x):
    return 1.0 / (1.0 + jnp.exp(-x))


def _layer_norm(z, g, b):
    mu = jnp.mean(z, axis=-1, keepdims=True)
    d = z - mu
    var = jnp.mean(d * d, axis=-1, keepdims=True)
    return d * lax.rsqrt(var + LN_EPS) * g + b


IN_PROJ_ROWS = 256


def _store_tail(tail_ref, acc, r0, tm):
    keep = tail_ref.shape[0]
    n = acc.shape[0]
    if r0 + n > tm - keep:
        lo = max(r0, tm - keep)
        tail_ref[lo - (tm - keep):r0 + n - (tm - keep), :] = acc[lo - r0:, :]


def _in_proj_kernel(x_ref, w_ref, h_ref, k_ref, v_ref, xb_ref, tail_ref):
    j = pl.program_id(1)
    tm = x_ref.shape[0]

    @pl.when(j == 0)
    def _():
        xb_ref[...] = x_ref[...].astype(BF16)

    is_sig = (j == 2) | (j == 3)
    is_silu = (j == 4) | (j == 5) | (j == COL_AG) | (j == COL_MG)
    n_rows = min(IN_PROJ_ROWS, tm)
    for r0 in range(0, tm, n_rows):
        rows = slice(r0, r0 + n_rows)
        acc = jnp.dot(xb_ref[rows, :], w_ref[0], preferred_element_type=F32)
        sg = _sigmoid(acc)
        out = jnp.where(is_sig, sg, jnp.where(is_silu, acc * sg, acc))
        h_ref[rows, :] = out.astype(BF16)
        _store_tail(tail_ref, acc, r0, tm)

    @pl.when(j == COL_K)
    def _():
        k_ref[...] = tail_ref[...]

    @pl.when(j == COL_V)
    def _():
        v_ref[...] = tail_ref[...]


def _in_proj(x, w, layer, *, seq_rows, keep):
    m = x.shape[0]
    tm = min(1024, m)
    assert m % tm == 0 and seq_rows % tm == 0 and keep <= tm and tm % min(IN_PROJ_ROWS, tm) == 0
    tiles_per_seq = seq_rows // tm
    n_seq = m // seq_rows
    keep_spec = pl.BlockSpec((keep, D_ATT), lambda i, j: (i // tiles_per_seq, 0))
    return pl.pallas_call(
        _in_proj_kernel,
        grid=(m // tm, N_COL_BLOCKS),
        in_specs=[
            pl.BlockSpec((tm, D_MODEL), lambda i, j: (i, 0)),
            pl.BlockSpec((1, D_MODEL, D_ATT), lambda i, j: (layer, 0, j)),
        ],
        out_specs=[pl.BlockSpec((tm, D_ATT), lambda i, j: (i, j)), keep_spec, keep_spec],
        out_shape=[
            jax.ShapeDtypeStruct((m, D_IN), BF16),
            jax.ShapeDtypeStruct((n_seq * keep, D_ATT), F32),
            jax.ShapeDtypeStruct((n_seq * keep, D_ATT), F32),
        ],
        scratch_shapes=[pltpu.VMEM((tm, D_MODEL), BF16), pltpu.VMEM((keep, D_ATT), F32)],
        compiler_params=_compiler_params(2),
        name="in_proj",
    )(x, w)


TAP_ROWS = 32


def _conv_offsets(s):
    return [o for o in range(HALO - CONV_LEFT, HALO + 1) if o % SUBLANES == s]


def _in_proj_conv_kernel(x_ref, w_ref, ctx_ref, cw_ref, cb_ref, h_ref, y_ref, st_ref, k_ref, v_ref,
                         xb_ref, tail_ref, up_ref, *, tiles_per_seq):
    i = pl.program_id(0)
    j = pl.program_id(1)
    tm = x_ref.shape[0]
    n_rows = min(IN_PROJ_ROWS, tm)
    seq_start = i % tiles_per_seq == 0

    @pl.when(j == 0)
    def _():
        xb_ref[...] = x_ref[...].astype(BF16)

    @pl.when((j == 0) & seq_start)
    def _():
        up_ref[0:HALO, :] = ctx_ref[0]

    @pl.when((j == 0) & jnp.logical_not(seq_start))
    def _():
        up_ref[0:HALO, :] = up_ref[tm:tm + HALO, :]

    def matmul_blocks(epilogue):
        for r0 in range(0, tm, n_rows):
            acc = jnp.dot(xb_ref[r0:r0 + n_rows, :], w_ref[0], preferred_element_type=F32)
            epilogue(r0, acc)

    for jj in range(N_GLU_BLOCKS):
        cols = slice((jj % 2) * D_ATT, (jj % 2 + 1) * D_ATT)

        @pl.when(j == jj)
        def _(jj=jj, cols=cols):
            def glu(r0, acc):
                rows = slice(HALO + r0, HALO + r0 + n_rows)
                if jj < 2:
                    up_ref[rows, cols] = acc
                else:
                    up_ref[rows, cols] = up_ref[rows, cols] * _sigmoid(acc)
            matmul_blocks(glu)

    @pl.when(j == N_GLU_BLOCKS)
    def _():
        st_ref[0] = up_ref[HALO + tm - CONV_LEFT:HALO + tm, :]

    is_silu = (j == 4) | (j == 5) | (j == COL_AG) | (j == COL_MG)

    def matmul_and_conv(apply_silu):
        lane = pl.ds(pl.multiple_of((j - N_GLU_BLOCKS) * LANES, LANES), LANES)
        groups = TAP_ROWS // SUBLANES

        def conv_rows(r0):
            base = up_ref[r0:r0 + HALO + TAP_ROWS, lane]
            parts = [None, None]
            for s in range(SUBLANES):
                offs = _conv_offsets(s)
                lo = offs[0] - s
                win = base[s + lo:offs[-1] + TAP_ROWS]
                for o in offs:
                    tap = win[o - s - lo:o - s - lo + TAP_ROWS]
                    w = cw_ref[o - (HALO - CONV_LEFT), :, lane]
                    term = tap.reshape(groups, SUBLANES, LANES) * w[None]
                    parts[s % 2] = term if parts[s % 2] is None else parts[s % 2] + term
            acc = (parts[0] + parts[1]).reshape(TAP_ROWS, LANES)
            y_ref[r0:r0 + TAP_ROWS, lane] = acc + cb_ref[0:1, lane]

        n_mm = tm // n_rows
        n_tap_blocks = tm // TAP_ROWS
        weights = [3] * (n_mm - 1) + [1]
        bounds = [n_tap_blocks * sum(weights[:r]) // sum(weights) for r in range(n_mm)] + [n_tap_blocks]
        for r in range(n_mm):
            r0 = r * n_rows
            acc = jnp.dot(xb_ref[r0:r0 + n_rows, :], w_ref[0], preferred_element_type=F32)
            for blk in range(bounds[r], bounds[r + 1]):
                conv_rows(blk * TAP_ROWS)
            out = acc * _sigmoid(acc) if apply_silu else acc
            h_ref[r0:r0 + n_rows, :] = out.astype(BF16)
            _store_tail(tail_ref, acc, r0, tm)

    @pl.when(is_silu)
    def _():
        matmul_and_conv(True)

    @pl.when((j >= N_GLU_BLOCKS) & jnp.logical_not(is_silu))
    def _():
        matmul_and_conv(False)

    @pl.when(j == COL_K)
    def _():
        k_ref[...] = tail_ref[...]

    @pl.when(j == COL_V)
    def _():
        v_ref[...] = tail_ref[...]


def _in_proj_conv(x, w, layer, ctx, cw, cb, *, seq_rows, keep):
    m = x.shape[0]
    tm = min(1024, m)
    assert m % tm == 0 and seq_rows % tm == 0 and keep <= tm and tm % min(IN_PROJ_ROWS, tm) == 0
    assert tm % TAP_ROWS == 0 and N_COL_BLOCKS - N_GLU_BLOCKS == D_CONV // LANES
    tiles_per_seq = seq_rows // tm
    n_seq = m // seq_rows
    seq_of = lambda i, j: i // tiles_per_seq
    keep_spec = pl.BlockSpec((keep, D_ATT), lambda i, j: (seq_of(i, j), 0))
    return pl.pallas_call(
        functools.partial(_in_proj_conv_kernel, tiles_per_seq=tiles_per_seq),
        grid=(m // tm, N_COL_BLOCKS),
        in_specs=[
            pl.BlockSpec((tm, D_MODEL), lambda i, j: (i, 0)),
            pl.BlockSpec((1, D_MODEL, D_ATT), lambda i, j: (layer, 0, j)),
            pl.BlockSpec((1, HALO, D_CONV), lambda i, j: (seq_of(i, j), 0, 0)),
            pl.BlockSpec((CONV_W, SUBLANES, D_CONV), lambda i, j: (0, 0, 0), pipeline_mode=ONCE),
            pl.BlockSpec((8, D_CONV), lambda i, j: (0, 0), pipeline_mode=ONCE),
        ],
        out_specs=[
            pl.BlockSpec((tm, D_ATT), lambda i, j: (i, jnp.maximum(j - N_GLU_BLOCKS, 0))),
            pl.BlockSpec((tm, D_CONV), lambda i, j: (i, 0)),
            pl.BlockSpec((1, CONV_LEFT, D_CONV), lambda i, j: (seq_of(i, j), 0, 0)),
            keep_spec, keep_spec,
        ],
        out_shape=[
            jax.ShapeDtypeStruct((m, D_IN - N_GLU_BLOCKS * D_ATT), BF16),
            jax.ShapeDtypeStruct((m, D_CONV), F32),
            jax.ShapeDtypeStruct((n_seq, CONV_LEFT, D_CONV), F32),
            jax.ShapeDtypeStruct((n_seq * keep, D_ATT), F32),
            jax.ShapeDtypeStruct((n_seq * keep, D_ATT), F32),
        ],
        scratch_shapes=[
            pltpu.VMEM((tm, D_MODEL), BF16),
            pltpu.VMEM((keep, D_ATT), F32),
            pltpu.VMEM((HALO + tm, D_CONV), F32),
        ],
        compiler_params=pltpu.CompilerParams(
            dimension_semantics=("arbitrary", "arbitrary"), vmem_limit_bytes=60 * 1024 * 1024),
        name="in_proj_conv",
    )(x, w, ctx, cw, cb)


def _mem_kv_kernel(x_ref, w_ref, k_ref, v_ref, kb_ref, vb_ref):
    acc = jnp.dot(x_ref[...], w_ref[0], preferred_element_type=F32)
    k = acc[:, :D_ATT]
    v = acc[:, D_ATT:]
    k_ref[0] = k
    v_ref[0] = v
    kb_ref[0] = k.astype(BF16)
    vb_ref[0] = v.astype(BF16)


def _mem_kv(mem, w):
    m = mem.shape[0]
    depth = w.shape[0]
    tm = min(1024, m)
    assert m % tm == 0
    out_spec = pl.BlockSpec((1, tm, D_ATT), lambda i, l: (l, i, 0))
    return pl.pallas_call(
        _mem_kv_kernel,
        grid=(m // tm, depth),
        in_specs=[
            pl.BlockSpec((tm, D_MODEL), lambda i, l: (i, 0)),
            pl.BlockSpec((1, D_MODEL, 2 * D_ATT), lambda i, l: (l, 0, 0)),
        ],
        out_specs=[out_spec] * 4,
        out_shape=[jax.ShapeDtypeStruct((depth, m, D_ATT), F32)] * 2
        + [jax.ShapeDtypeStruct((depth, m, D_ATT), BF16)] * 2,
        compiler_params=_compiler_params(2),
        name="mem_kv",
    )(mem, w)


CONV_ROWS = 32
CONV_LANES = 256


def _conv_kernel(a_ref, sb_ref, g_ref, ah_ref, sbh_ref, ctx_ref, cw_ref, cp_ref, wpw_ref,
                 o_ref, st_ref, up_ref, us_ref, y_ref, *, tq, n_t):
    t = pl.program_id(1)

    up_ref[HALO:HALO + tq, :] = a_ref[...].astype(F32) * sb_ref[...].astype(F32)

    @pl.when(t == 0)
    def _():
        up_ref[0:HALO, :] = ctx_ref[0]

    @pl.when(t > 0)
    def _():
        up_ref[0:HALO, :] = ah_ref[...].astype(F32) * sbh_ref[...].astype(F32)

    n_shifted = tq + HALO - SUBLANES
    for s in range(1, SUBLANES):
        us_ref[s - 1, 0:n_shifted, :] = up_ref[s:s + n_shifted, :]

    groups = CONV_ROWS // SUBLANES
    for lc in range(D_CONV // CONV_LANES):
        lanes = slice(lc * CONV_LANES, (lc + 1) * CONV_LANES)

        def row_block(rb, carry, lanes=lanes):
            r0 = pl.multiple_of(rb * CONV_ROWS, CONV_ROWS)
            parts = [None, None]
            for s in range(SUBLANES):
                offs = _conv_offsets(s)
                lo = offs[0] - s
                rows = pl.ds(r0 + lo, offs[-1] - s + CONV_ROWS - lo)
                win = up_ref[rows, lanes] if s == 0 else us_ref[s - 1, rows, lanes]
                for o in offs:
                    tap = win[o - s - lo:o - s - lo + CONV_ROWS]
                    w = cw_ref[o - (HALO - CONV_LEFT), :, lanes]
                    term = tap.reshape(groups, SUBLANES, CONV_LANES) * w[None]
                    parts[s % 2] = term if parts[s % 2] is None else parts[s % 2] + term
            acc = parts[0] + parts[1]
            y_ref[pl.ds(r0, CONV_ROWS), lanes] = acc.reshape(CONV_ROWS, CONV_LANES) + cp_ref[0:1, lanes]
            return carry

        lax.fori_loop(0, tq // CONV_ROWS, row_block, 0)

    yn = _layer_norm(y_ref[...], cp_ref[1:2, :], cp_ref[2:3, :])
    sw = yn * _sigmoid(yn)
    pw = jnp.dot(sw.astype(BF16), wpw_ref[0], preferred_element_type=F32)
    o_ref[...] = (pw * g_ref[...].astype(F32)).astype(BF16)

    @pl.when(t == n_t - 1)
    def _():
        st_ref[0] = up_ref[HALO + tq - CONV_LEFT:HALO + tq, :]


def _conv_branch(h, ctx, cw, cp, wpw, layer, *, n_seq, t_len, tq):
    assert t_len % tq == 0 and tq % HALO == 0 and tq >= HALO
    n_t = t_len // tq
    m = n_seq * t_len

    def row_map(col):
        return lambda b, t: (b * n_t + t, col)

    def halo_map(col):
        return lambda b, t: (jnp.maximum((b * t_len + t * tq) // HALO - 1, 0), col)

    return pl.pallas_call(
        functools.partial(_conv_kernel, tq=tq, n_t=n_t),
        grid=(n_seq, n_t),
        in_specs=[
            pl.BlockSpec((tq, D_CONV), row_map(0)),
            pl.BlockSpec((tq, D_CONV), row_map(1)),
            pl.BlockSpec((tq, D_CONV), row_map(2)),
            pl.BlockSpec((HALO, D_CONV), halo_map(0)),
            pl.BlockSpec((HALO, D_CONV), halo_map(1)),
            pl.BlockSpec((1, HALO, D_CONV), lambda b, t: (b, 0, 0)),
            pl.BlockSpec((CONV_W, SUBLANES, D_CONV), lambda b, t: (0, 0, 0)),
            pl.BlockSpec((8, D_CONV), lambda b, t: (0, 0)),
            pl.BlockSpec((1, D_CONV, D_CONV), lambda b, t: (layer, 0, 0)),
        ],
        out_specs=[
            pl.BlockSpec((tq, D_CONV), lambda b, t: (b * n_t + t, 0)),
            pl.BlockSpec((1, CONV_LEFT, D_CONV), lambda b, t: (b, 0, 0)),
        ],
        out_shape=[
            jax.ShapeDtypeStruct((m, D_CONV), BF16),
            jax.ShapeDtypeStruct((n_seq, CONV_LEFT, D_CONV), F32),
        ],
        scratch_shapes=[
            pltpu.VMEM((HALO + tq, D_CONV), F32),
            pltpu.VMEM((SUBLANES - 1, HALO + tq, D_CONV), F32),
            pltpu.VMEM((tq, D_CONV), F32),
        ],
        compiler_params=_compiler_params(2),
        name="conv_branch",
    )(h, h, h, h, h, ctx, cw, cp, wpw)


def _qk(q, k):
    return lax.dot_general(q, k, (((1,), (1,)), ((), ())), preferred_element_type=F32)


def _attend(qs, ks, vs, biases):
    ss = [_qk(q, k) * SM_SCALE for q, k in zip(qs, ks)]
    ss = [s if b is None else s + b for s, b in zip(ss, biases)]
    ps, ls = [], []
    for s in ss:
        m = jnp.max(s, axis=-1, keepdims=True)
        p = jnp.exp(s - m)
        ls.append(jnp.sum(p, axis=-1, keepdims=True))
        ps.append(p.astype(BF16))
    return [jnp.dot(p, v, preferred_element_type=F32) / l for p, v, l in zip(ps, vs, ls)]


def _head_cols(h):
    return slice(h * HEAD_DIM, (h + 1) * HEAD_DIM)


def _memory_attention(mq_ref, mk_ref, mv_ref, mg_ref, o_ref):
    heads = range(N_HEADS)
    os = _attend([mq_ref[:, _head_cols(h)] for h in heads], [mk_ref[0, 0, :, _head_cols(h)] for h in heads],
                 [mv_ref[0, 0, :, _head_cols(h)] for h in heads], [None] * N_HEADS)
    for h in heads:
        o_ref[:, D_ATT + h * HEAD_DIM:D_ATT + (h + 1) * HEAD_DIM] = (
            os[h] * mg_ref[:, _head_cols(h)].astype(F32)).astype(BF16)


def _attn_prompt_kernel(q_ref, k_ref, v_ref, ag_ref, mq_ref, mg_ref, mk_ref, mv_ref, bias_ref,
                        o_ref, *, tq):
    t = pl.program_id(1)
    n_q = tq // Q_TILE
    heads = range(N_HEADS)

    def q_tile(qi, carry):
        c0 = (t * n_q + qi) * Q_CHUNKS
        start = pl.multiple_of(jnp.maximum(c0 * CHUNK - ATT_WIN, 0), Q_TILE)
        variant = jnp.minimum(c0 // Q_CHUNKS, N_BIAS_VARIANTS - 1)
        rows = pl.ds(pl.multiple_of(qi * Q_TILE, Q_TILE), Q_TILE)
        band = pl.ds(start, Q_BAND)
        os = _attend([q_ref[rows, _head_cols(h)] for h in heads], [k_ref[band, _head_cols(h)] for h in heads],
                     [v_ref[band, _head_cols(h)] for h in heads], [bias_ref[variant, h] for h in heads])
        for h in heads:
            o_ref[rows, _head_cols(h)] = (os[h] * ag_ref[rows, _head_cols(h)].astype(F32)).astype(BF16)
        return carry

    lax.fori_loop(0, n_q, q_tile, 0)
    _memory_attention(mq_ref, mk_ref, mv_ref, mg_ref, o_ref)


def _attn_prompt(h, mk, mv, layer, bias, *, n_seq, t_len, tq, col0):
    assert t_len % tq == 0 and tq % Q_TILE == 0 and t_len >= Q_BAND
    n_t = t_len // tq

    def row_map(col):
        return lambda b, t: (b * n_t + t, col0 + col)

    return pl.pallas_call(
        functools.partial(_attn_prompt_kernel, tq=tq),
        grid=(n_seq, n_t),
        in_specs=[
            pl.BlockSpec((tq, D_ATT), row_map(COL_Q)),
            pl.BlockSpec((t_len, D_ATT), lambda b, t: (b, col0 + COL_K)),
            pl.BlockSpec((t_len, D_ATT), lambda b, t: (b, col0 + COL_V)),
            pl.BlockSpec((tq, D_ATT), row_map(COL_AG)),
            pl.BlockSpec((tq, D_ATT), row_map(COL_MQ)),
            pl.BlockSpec((tq, D_ATT), row_map(COL_MG)),
            pl.BlockSpec((1, 1, N_MEM, D_ATT), lambda b, t: (layer, b, 0, 0)),
            pl.BlockSpec((1, 1, N_MEM, D_ATT), lambda b, t: (layer, b, 0, 0)),
            pl.BlockSpec(bias.shape, lambda b, t: (0, 0, 0, 0)),
        ],
        out_specs=pl.BlockSpec((tq, 2 * D_ATT), lambda b, t: (b * n_t + t, 0)),
        out_shape=jax.ShapeDtypeStruct((n_seq * t_len, 2 * D_ATT), BF16),
        compiler_params=_compiler_params(2),
        name="attn_prompt",
    )(h, h, h, h, h, h, mk, mv, bias)


def _attn_sample_kernel(q_ref, k_ref, v_ref, ag_ref, mq_ref, mg_ref, kc_ref, vc_ref, mk_ref, mv_ref,
                        bias_ref, o_ref):
    heads = range(N_HEADS)
    ks = [jnp.concatenate([kc_ref[0, 0, :, _head_cols(h)], k_ref[:, _head_cols(h)]], axis=0) for h in heads]
    vs = [jnp.concatenate([vc_ref[0, 0, :, _head_cols(h)], v_ref[:, _head_cols(h)]], axis=0) for h in heads]
    os = _attend([q_ref[:, _head_cols(h)] for h in heads], ks, vs, [bias_ref[h] for h in heads])
    for h in heads:
        o_ref[:, _head_cols(h)] = (os[h] * ag_ref[:, _head_cols(h)].astype(F32)).astype(BF16)
    _memory_attention(mq_ref, mk_ref, mv_ref, mg_ref, o_ref)


def _attn_sample(h, kc, vc, mk, mv, layer, bias, *, n_seq):
    def row_map(col):
        return lambda b: (b, col)

    seq_spec = lambda n: pl.BlockSpec((1, 1, n, D_ATT), lambda b: (layer, b, 0, 0))
    return pl.pallas_call(
        _attn_sample_kernel,
        grid=(n_seq,),
        in_specs=[
            pl.BlockSpec((CHUNK, D_ATT), row_map(COL_Q)),
            pl.BlockSpec((CHUNK, D_ATT), row_map(COL_K)),
            pl.BlockSpec((CHUNK, D_ATT), row_map(COL_V)),
            pl.BlockSpec((CHUNK, D_ATT), row_map(COL_AG)),
            pl.BlockSpec((CHUNK, D_ATT), row_map(COL_MQ)),
            pl.BlockSpec((CHUNK, D_ATT), row_map(COL_MG)),
            seq_spec(ATT_WIN), seq_spec(ATT_WIN), seq_spec(N_MEM), seq_spec(N_MEM),
            pl.BlockSpec(bias.shape, lambda b: (0, 0, 0)),
        ],
        out_specs=pl.BlockSpec((CHUNK, 2 * D_ATT), lambda b: (b, 0)),
        out_shape=jax.ShapeDtypeStruct((n_seq * CHUNK, 2 * D_ATT), BF16),
        compiler_params=_compiler_params(1),
        name="attn_sample",
    )(h, h, h, h, h, h, kc, vc, mk, mv, bias)


OUT_COLS = 512


def _out_proj_kernel(cm_ref, am_ref, x_ref, w_ref, gb_ref, y_ref, z_ref, *, alpha):
    tm = x_ref.shape[0]
    row_sum = jnp.zeros((tm, 1), F32)
    for n in range(D_MODEL // OUT_COLS):
        cols = slice(n * OUT_COLS, (n + 1) * OUT_COLS)
        z = jnp.dot(cm_ref[...], w_ref[0, 0:D_CONV, cols], preferred_element_type=F32)
        z = z + jnp.dot(am_ref[...], w_ref[0, D_CONV:D_MIX, cols], preferred_element_type=F32)
        z = alpha * x_ref[:, cols] + z
        z_ref[:, cols] = z
        row_sum = row_sum + jnp.sum(z, axis=-1, keepdims=True)
    mu = row_sum * (1.0 / D_MODEL)
    sq_sum = jnp.zeros((tm, 1), F32)
    for n in range(D_MODEL // OUT_COLS):
        cols = slice(n * OUT_COLS, (n + 1) * OUT_COLS)
        d = z_ref[:, cols] - mu
        sq_sum = sq_sum + jnp.sum(d * d, axis=-1, keepdims=True)
    inv = lax.rsqrt(sq_sum * (1.0 / D_MODEL) + LN_EPS)
    for n in range(D_MODEL // OUT_COLS):
        cols = slice(n * OUT_COLS, (n + 1) * OUT_COLS)
        y_ref[:, cols] = (z_ref[:, cols] - mu) * inv * gb_ref[0:1, cols] + gb_ref[1:2, cols]


def _out_proj(cm, am, x, w, layer, gb, *, alpha):
    m = x.shape[0]
    tm = min(512, m)
    assert m % tm == 0
    return pl.pallas_call(
        functools.partial(_out_proj_kernel, alpha=alpha),
        grid=(m // tm,),
        in_specs=[
            pl.BlockSpec((tm, D_CONV), lambda i: (i, 0)),
            pl.BlockSpec((tm, 2 * D_ATT), lambda i: (i, 0)),
            pl.BlockSpec((tm, D_MODEL), lambda i: (i, 0)),
            pl.BlockSpec((1, D_MIX, D_MODEL), lambda i: (layer, 0, 0)),
            pl.BlockSpec((8, D_MODEL), lambda i: (0, 0)),
        ],
        out_specs=pl.BlockSpec((tm, D_MODEL), lambda i: (i, 0)),
        out_shape=jax.ShapeDtypeStruct((m, D_MODEL), F32),
        scratch_shapes=[pltpu.VMEM((tm, D_MODEL), F32)],
        compiler_params=_compiler_params(1),
        name="out_proj",
    )(cm, am, x, w, gb)


OUT_ROWS = 256


def _out_proj_conv_kernel(yc_ref, g_ref, am_ref, x_ref, cp_ref, wpw_ref, w_ref, gb_ref, y_ref, *, alpha):
    tm = x_ref.shape[0]
    n_rows = min(OUT_ROWS, tm)
    for r0 in range(0, tm, n_rows):
        rows = slice(r0, r0 + n_rows)
        yn = _layer_norm(yc_ref[rows, :], cp_ref[1:2, :], cp_ref[2:3, :])
        sw = yn * _sigmoid(yn)
        pw = jnp.dot(sw.astype(BF16), wpw_ref[0], preferred_element_type=F32)
        cm = (pw * g_ref[rows, :].astype(F32)).astype(BF16)
        z = jnp.dot(cm, w_ref[0, 0:D_CONV, :], preferred_element_type=F32)
        z = z + jnp.dot(am_ref[rows, :], w_ref[0, D_CONV:D_MIX, :], preferred_element_type=F32)
        z = z + alpha * x_ref[rows, :]
        y_ref[rows, :] = _layer_norm(z, gb_ref[0:1, :], gb_ref[1:2, :])


def _out_proj_conv(yc, h, am, x, cp, wpw, w, layer, gb, *, alpha):
    m = x.shape[0]
    tm = min(512, m)
    assert m % tm == 0 and tm % min(OUT_ROWS, tm) == 0
    rows_of = lambda width: pl.BlockSpec((tm, width), lambda i: (i, 0))
    whole = lambda shape: pl.BlockSpec(shape, lambda i: (0, 0), pipeline_mode=ONCE)
    of_layer = lambda shape: pl.BlockSpec((1,) + shape, lambda i: (layer, 0, 0), pipeline_mode=ONCE)
    return pl.pallas_call(
        functools.partial(_out_proj_conv_kernel, alpha=alpha),
        grid=(m // tm,),
        in_specs=[
            rows_of(D_CONV), rows_of(D_CONV), rows_of(2 * D_ATT), rows_of(D_MODEL),
            whole((8, D_CONV)), of_layer((D_CONV, D_CONV)), of_layer((D_MIX, D_MODEL)), whole((8, D_MODEL)),
        ],
        out_specs=rows_of(D_MODEL),
        out_shape=jax.ShapeDtypeStruct((m, D_MODEL), F32),
        compiler_params=_compiler_params(1),
        name="out_proj_conv",
    )(yc, h, am, x, cp, wpw, w, gb)


def _band_bias(table):
    n_h = table.shape[0]
    n_u = Q_TILE + Q_BAND - 1
    n_flat = ATT_WIN + Q_TILE - REL_CLIP
    lo = REL_CLIP - (Q_TILE - 1)
    assert lo >= 0 and n_flat + 2 * REL_CLIP - lo == n_u
    u = jnp.concatenate([jnp.broadcast_to(table[:, 2 * REL_CLIP:], (n_h, n_flat)),
                         jnp.flip(table[:, lo:2 * REL_CLIP], axis=1)], axis=1).astype(F32)
    u = jnp.pad(u, ((0, 0), (0, 1)))
    skew = jnp.broadcast_to(u[:, None, :], (n_h, Q_TILE, n_u + 1)).reshape(n_h, Q_TILE * (n_u + 1))
    full = skew[:, Q_TILE - 1:Q_TILE - 1 + Q_TILE * n_u].reshape(n_h, Q_TILE, n_u)[:, :, :Q_BAND]
    qc = jnp.arange(Q_TILE)[:, None] // CHUNK
    kc = jnp.arange(Q_BAND)[None, :] // CHUNK
    full = jnp.where((kc >= qc) & (kc <= qc + N_LEFT_CHUNKS), full, MASK_VALUE)
    variants = []
    for v in range(N_BIAS_VARIANTS - 1):
        off = ATT_WIN - v * Q_TILE
        pad = jnp.full((n_h, Q_TILE, off), MASK_VALUE, F32)
        variants.append(jnp.concatenate([full[:, :, off:], pad], axis=-1))
    variants.append(full)
    return jnp.stack(variants)


def _pack_rows(rows, width):
    return jnp.concatenate([jnp.stack(rows).astype(F32), jnp.zeros((8 - len(rows), width), F32)])


def kernel(x_prompt, x_sample, mem_prompt, cache_conv, cache_att_k, cache_att_v, cache_mem_k, cache_mem_v,
           w_in, conv_w, conv_b, conv_ln_g, conv_ln_b, w_pw, rel_table, w_mem_kv, w_out, ln_g, ln_b):
    n_p, t_p, _ = x_prompt.shape
    n_s, t_s, _ = x_sample.shape
    depth = w_in.shape[0]
    n_mem = mem_prompt.shape[1]
    assert t_s == CHUNK and n_mem == N_MEM and cache_att_k.shape[2] == ATT_WIN
    alpha = (2 * depth) ** 0.25
    keep = min(ATT_WIN, t_p)

    w_in_b = w_in.astype(BF16)
    w_pw_b = w_pw.astype(BF16)
    w_out_b = w_out.astype(BF16)
    w_mkv_b = w_mem_kv.astype(BF16)

    mk_p, mv_p, mk_pb, mv_pb = _mem_kv(mem_prompt.reshape(n_p * n_mem, D_MODEL).astype(BF16), w_mkv_b)
    mk_pb = mk_pb.reshape(depth, n_p, n_mem, D_ATT)
    mv_pb = mv_pb.reshape(depth, n_p, n_mem, D_ATT)

    xp = x_prompt.reshape(n_p * t_p, D_MODEL)
    xs = x_sample.reshape(n_s * t_s, D_MODEL)
    zero_ctx = jnp.zeros((n_p, HALO, D_CONV), F32)
    cache_ctx = jnp.pad(cache_conv, ((0, 0), (0, 0), (HALO - CONV_LEFT, 0), (0, 0)))
    kc_b = cache_att_k.reshape(depth, n_s, ATT_WIN, D_ATT).astype(BF16)
    vc_b = cache_att_v.reshape(depth, n_s, ATT_WIN, D_ATT).astype(BF16)
    mk_sb = cache_mem_k.reshape(depth, n_s, n_mem, D_ATT).astype(BF16)
    mv_sb = cache_mem_v.reshape(depth, n_s, n_mem, D_ATT).astype(BF16)

    conv_p, kp_l, vp_l, conv_s, ks_l, vs_l = [], [], [], [], [], []
    for l in range(depth):
        cw = jnp.broadcast_to(conv_w[l][:, None, :], (CONV_W, SUBLANES, D_CONV))
        cp = _pack_rows([conv_b[l], conv_ln_g[l], conv_ln_b[l]], D_CONV)
        gb = _pack_rows([ln_g[l], ln_b[l]], D_MODEL)
        bias = _band_bias(rel_table[l])

        h, yc, cs, k_new, v_new = _in_proj_conv(xp, w_in_b, l, zero_ctx, cw, cp, seq_rows=t_p, keep=keep)
        am = _attn_prompt(h, mk_pb, mv_pb, l, bias, n_seq=n_p, t_len=t_p, tq=512, col0=-N_GLU_BLOCKS)
        xp = _out_proj_conv(yc, h, am, xp, cp, w_pw_b, w_out_b, l, gb, alpha=alpha)
        conv_p.append(cs)
        kp_l.append(k_new)
        vp_l.append(v_new)

        h, k_new, v_new = _in_proj(xs, w_in_b, l, seq_rows=n_s * t_s, keep=n_s * t_s)
        cm, cs = _conv_branch(h, cache_ctx[l], cw, cp, w_pw_b, l, n_seq=n_s, t_len=t_s, tq=t_s)
        am = _attn_sample(h, kc_b, vc_b, mk_sb, mv_sb, l, bias[-1, :, :CHUNK, :BAND], n_seq=n_s)
        xs = _out_proj(cm, am, xs, w_out_b, l, gb, alpha=alpha)
        conv_s.append(cs)
        ks_l.append(k_new)
        vs_l.append(v_new)

    mem_shape = (depth, n_p, n_mem, N_HEADS, HEAD_DIM)
    kv_p_shape = (depth, n_p, keep, N_HEADS, HEAD_DIM)
    kv_s_shape = (depth, n_s, t_s, N_HEADS, HEAD_DIM)
    return (xp.reshape(n_p, t_p, D_MODEL), xs.reshape(n_s, t_s, D_MODEL), jnp.stack(conv_p),
            jnp.stack(kp_l).reshape(kv_p_shape), jnp.stack(vp_l).reshape(kv_p_shape),
            mk_p.reshape(mem_shape), mv_p.reshape(mem_shape),
            jnp.stack(conv_s), jnp.stack(ks_l).reshape(kv_s_shape), jnp.stack(vs_l).reshape(kv_s_shape))
```

```python
import functools

import jax
import jax.numpy as jnp
from jax import lax
from jax.experimental import pallas as pl
from jax.experimental.pallas import tpu as pltpu

D_MODEL = 2048
CHUNK = 64
N_LEFT_CHUNKS = 8
ATT_WIN = N_LEFT_CHUNKS * CHUNK
BAND = ATT_WIN + CHUNK
CONV_W = 31
CONV_LEFT = CONV_W - 1
D_CONV = D_MODEL // 2
HEAD_DIM = 128
N_HEADS = 4
D_ATT = N_HEADS * HEAD_DIM
N_MEM = 256
REL_CLIP = 128
D_MIX = D_CONV + 2 * D_ATT
D_IN = 3 * D_CONV + 6 * D_ATT
LN_EPS = 1e-5
SM_SCALE = HEAD_DIM ** -0.5
MASK_VALUE = -1e30

COL_Q, COL_K, COL_V, COL_AG, COL_MQ, COL_MG = 6, 7, 8, 9, 10, 11
N_COL_BLOCKS = D_IN // D_ATT
N_GLU_BLOCKS = 4

LANES = 128
SUBLANES = 8
HALO = 32

Q_CHUNKS = 2
Q_TILE = Q_CHUNKS * CHUNK
Q_BAND = ATT_WIN + Q_TILE
N_BIAS_VARIANTS = N_LEFT_CHUNKS // Q_CHUNKS + 1

BF16 = jnp.bfloat16
F32 = jnp.float32

VMEM_LIMIT_BYTES = 56 * 1024 * 1024
ONCE = pl.Buffered(1)


def _compiler_params(n_axes):
    return pltpu.CompilerParams(
        dimension_semantics=("arbitrary",) * n_axes, vmem_limit_bytes=VMEM_LIMIT_BYTES)


def _sigmoid(x):
    return 1.0 / (1.0 + jnp.exp(-x))


def _layer_norm(z, g, b):
    mu = jnp.mean(z, axis=-1, keepdims=True)
    d = z - mu
    var = jnp.mean(d * d, axis=-1, keepdims=True)
    return d * lax.rsqrt(var + LN_EPS) * g + b


IN_PROJ_ROWS = 256


def _store_tail(tail_ref, acc, r0, tm):
    keep = tail_ref.shape[0]
    n = acc.shape[0]
    if r0 + n > tm - keep:
        lo = max(r0, tm - keep)
        tail_ref[lo - (tm - keep):r0 + n - (tm - keep), :] = acc[lo - r0:, :]


def _in_proj_kernel(x_ref, w_ref, h_ref, k_ref, v_ref, xb_ref, tail_ref):
    j = pl.program_id(1)
    tm = x_ref.shape[0]

    @pl.when(j == 0)
    def _():
        xb_ref[...] = x_ref[...].astype(BF16)

    is_sig = (j == 2) | (j == 3)
    is_silu = (j == 4) | (j == 5) | (j == COL_AG) | (j == COL_MG)
    n_rows = min(IN_PROJ_ROWS, tm)
    for r0 in range(0, tm, n_rows):
        rows = slice(r0, r0 + n_rows)
        acc = jnp.dot(xb_ref[rows, :], w_ref[0], preferred_element_type=F32)
        sg = _sigmoid(acc)
        out = jnp.where(is_sig, sg, jnp.where(is_silu, acc * sg, acc))
        h_ref[rows, :] = out.astype(BF16)
        _store_tail(tail_ref, acc, r0, tm)

    @pl.when(j == COL_K)
    def _():
        k_ref[...] = tail_ref[...]

    @pl.when(j == COL_V)
    def _():
        v_ref[...] = tail_ref[...]


def _in_proj(x, w, layer, *, seq_rows, keep):
    m = x.shape[0]
    tm = min(1024, m)
    assert m % tm == 0 and seq_rows % tm == 0 and keep <= tm and tm % min(IN_PROJ_ROWS, tm) == 0
    tiles_per_seq = seq_rows // tm
    n_seq = m // seq_rows
    keep_spec = pl.BlockSpec((keep, D_ATT), lambda i, j: (i // tiles_per_seq, 0))
    return pl.pallas_call(
        _in_proj_kernel,
        grid=(m // tm, N_COL_BLOCKS),
        in_specs=[
            pl.BlockSpec((tm, D_MODEL), lambda i, j: (i, 0)),
            pl.BlockSpec((1, D_MODEL, D_ATT), lambda i, j: (layer, 0, j)),
        ],
        out_specs=[pl.BlockSpec((tm, D_ATT), lambda i, j: (i, j)), keep_spec, keep_spec],
        out_shape=[
            jax.ShapeDtypeStruct((m, D_IN), BF16),
            jax.ShapeDtypeStruct((n_seq * keep, D_ATT), F32),
            jax.ShapeDtypeStruct((n_seq * keep, D_ATT), F32),
        ],
        scratch_shapes=[pltpu.VMEM((tm, D_MODEL), BF16), pltpu.VMEM((keep, D_ATT), F32)],
        compiler_params=_compiler_params(2),
        name="in_proj",
    )(x, w)


TAP_ROWS = 32


def _conv_offsets(s):
    return [o for o in range(HALO - CONV_LEFT, HALO + 1) if o % SUBLANES == s]


def _in_proj_conv_kernel(x_ref, w_ref, ctx_ref, cw_ref, cb_ref, h_ref, y_ref, st_ref, k_ref, v_ref,
                         xb_ref, tail_ref, up_ref, *, tiles_per_seq):
    i = pl.program_id(0)
    j = pl.program_id(1)
    tm = x_ref.shape[0]
    n_rows = min(IN_PROJ_ROWS, tm)
    seq_start = i % tiles_per_seq == 0

    @pl.when(j == 0)
    def _():
        xb_ref[...] = x_ref[...].astype(BF16)

    @pl.when((j == 0) & seq_start)
    def _():
        up_ref[0:HALO, :] = ctx_ref[0]

    @pl.when((j == 0) & jnp.logical_not(seq_start))
    def _():
        up_ref[0:HALO, :] = up_ref[tm:tm + HALO, :]

    def matmul_blocks(epilogue):
        for r0 in range(0, tm, n_rows):
            acc = jnp.dot(xb_ref[r0:r0 + n_rows, :], w_ref[0], preferred_element_type=F32)
            epilogue(r0, acc)

    for jj in range(N_GLU_BLOCKS):
        cols = slice((jj % 2) * D_ATT, (jj % 2 + 1) * D_ATT)

        @pl.when(j == jj)
        def _(jj=jj, cols=cols):
            def glu(r0, acc):
                rows = slice(HALO + r0, HALO + r0 + n_rows)
                if jj < 2:
                    up_ref[rows, cols] = acc
                else:
                    up_ref[rows, cols] = up_ref[rows, cols] * _sigmoid(acc)
            matmul_blocks(glu)

    @pl.when(j == N_GLU_BLOCKS)
    def _():
        st_ref[0] = up_ref[HALO + tm - CONV_LEFT:HALO + tm, :]

    is_silu = (j == 4) | (j == 5) | (j == COL_AG) | (j == COL_MG)

    def matmul_and_conv(apply_silu):
        lane = pl.ds(pl.multiple_of((j - N_GLU_BLOCKS) * LANES, LANES), LANES)
        groups = TAP_ROWS // SUBLANES

        def conv_rows(r0):
            base = up_ref[r0:r0 + HALO + TAP_ROWS, lane]
            parts = [None, None]
            for s in range(SUBLANES):
                offs = _conv_offsets(s)
                lo = offs[0] - s
                win = base[s + lo:offs[-1] + TAP_ROWS]
                for o in offs:
                    tap = win[o - s - lo:o - s - lo + TAP_ROWS]
                    w = cw_ref[o - (HALO - CONV_LEFT), :, lane]
                    term = tap.reshape(groups, SUBLANES, LANES) * w[None]
                    parts[s % 2] = term if parts[s % 2] is None else parts[s % 2] + term
            acc = (parts[0] + parts[1]).reshape(TAP_ROWS, LANES)
            y_ref[r0:r0 + TAP_ROWS, lane] = acc + cb_ref[0:1, lane]

        n_mm = tm // n_rows
        n_tap_blocks = tm // TAP_ROWS
        weights = [3] * (n_mm - 1) + [1]
        bounds = [n_tap_blocks * sum(weights[:r]) // sum(weights) for r in range(n_mm)] + [n_tap_blocks]
        for r in range(n_mm):
            r0 = r * n_rows
            acc = jnp.dot(xb_ref[r0:r0 + n_rows, :], w_ref[0], preferred_element_type=F32)
            for blk in range(bounds[r], bounds[r + 1]):
                conv_rows(blk * TAP_ROWS)
            out = acc * _sigmoid(acc) if apply_silu else acc
            h_ref[r0:r0 + n_rows, :] = out.astype(BF16)
            if not apply_silu:
                _store_tail(tail_ref, acc, r0, tm)

    @pl.when(is_silu)
    def _():
        matmul_and_conv(True)

    @pl.when((j >= N_GLU_BLOCKS) & jnp.logical_not(is_silu))
    def _():
        matmul_and_conv(False)

    @pl.when(j == COL_K)
    def _():
        k_ref[...] = tail_ref[...]

    @pl.when(j == COL_V)
    def _():
        v_ref[...] = tail_ref[...]


def _in_proj_conv(x, w, layer, ctx, cw, cb, *, seq_rows, keep):
    m = x.shape[0]
    tm = min(1024, m)
    assert m % tm == 0 and seq_rows % tm == 0 and keep <= tm and tm % min(IN_PROJ_ROWS, tm) == 0
    assert tm % TAP_ROWS == 0 and N_COL_BLOCKS - N_GLU_BLOCKS == D_CONV // LANES
    tiles_per_seq = seq_rows // tm
    n_seq = m // seq_rows
    seq_of = lambda i, j: i // tiles_per_seq
    keep_spec = pl.BlockSpec((keep, D_ATT), lambda i, j: (seq_of(i, j), 0))
    return pl.pallas_call(
        functools.partial(_in_proj_conv_kernel, tiles_per_seq=tiles_per_seq),
        grid=(m // tm, N_COL_BLOCKS),
        in_specs=[
            pl.BlockSpec((tm, D_MODEL), lambda i, j: (i, 0)),
            pl.BlockSpec((1, D_MODEL, D_ATT), lambda i, j: (layer, 0, j)),
            pl.BlockSpec((1, HALO, D_CONV), lambda i, j: (seq_of(i, j), 0, 0)),
            pl.BlockSpec((CONV_W, SUBLANES, D_CONV), lambda i, j: (0, 0, 0), pipeline_mode=ONCE),
            pl.BlockSpec((8, D_CONV), lambda i, j: (0, 0), pipeline_mode=ONCE),
        ],
        out_specs=[
            pl.BlockSpec((tm, D_ATT), lambda i, j: (i, jnp.maximum(j - N_GLU_BLOCKS, 0))),
            pl.BlockSpec((tm, D_CONV), lambda i, j: (i, 0)),
            pl.BlockSpec((1, CONV_LEFT, D_CONV), lambda i, j: (seq_of(i, j), 0, 0)),
            keep_spec, keep_spec,
        ],
        out_shape=[
            jax.ShapeDtypeStruct((m, D_IN - N_GLU_BLOCKS * D_ATT), BF16),
            jax.ShapeDtypeStruct((m, D_CONV), F32),
            jax.ShapeDtypeStruct((n_seq, CONV_LEFT, D_CONV), F32),
            jax.ShapeDtypeStruct((n_seq * keep, D_ATT), F32),
            jax.ShapeDtypeStruct((n_seq * keep, D_ATT), F32),
        ],
        scratch_shapes=[
            pltpu.VMEM((tm, D_MODEL), BF16),
            pltpu.VMEM((keep, D_ATT), F32),
            pltpu.VMEM((HALO + tm, D_CONV), F32),
        ],
        compiler_params=pltpu.CompilerParams(
            dimension_semantics=("arbitrary", "arbitrary"), vmem_limit_bytes=60 * 1024 * 1024),
        name="in_proj_conv",
    )(x, w, ctx, cw, cb)


def _mem_kv_kernel(x_ref, w_ref, k_ref, v_ref, kb_ref, vb_ref):
    acc = jnp.dot(x_ref[...], w_ref[0], preferred_element_type=F32)
    k = acc[:, :D_ATT]
    v = acc[:, D_ATT:]
    k_ref[0] = k
    v_ref[0] = v
    kb_ref[0] = k.astype(BF16)
    vb_ref[0] = v.astype(BF16)


def _mem_kv(mem, w):
    m = mem.shape[0]
    depth = w.shape[0]
    tm = min(1024, m)
    assert m % tm == 0
    out_spec = pl.BlockSpec((1, tm, D_ATT), lambda i, l: (l, i, 0))
    return pl.pallas_call(
        _mem_kv_kernel,
        grid=(m // tm, depth),
        in_specs=[
            pl.BlockSpec((tm, D_MODEL), lambda i, l: (i, 0)),
            pl.BlockSpec((1, D_MODEL, 2 * D_ATT), lambda i, l: (l, 0, 0)),
        ],
        out_specs=[out_spec] * 4,
        out_shape=[jax.ShapeDtypeStruct((depth, m, D_ATT), F32)] * 2
        + [jax.ShapeDtypeStruct((depth, m, D_ATT), BF16)] * 2,
        compiler_params=_compiler_params(2),
        name="mem_kv",
    )(mem, w)


CONV_ROWS = 32
CONV_LANES = 256


def _conv_kernel(a_ref, sb_ref, g_ref, ah_ref, sbh_ref, ctx_ref, cw_ref, cp_ref, wpw_ref,
                 o_ref, st_ref, up_ref, us_ref, y_ref, *, tq, n_t):
    t = pl.program_id(1)

    up_ref[HALO:HALO + tq, :] = a_ref[...].astype(F32) * sb_ref[...].astype(F32)

    @pl.when(t == 0)
    def _():
        up_ref[0:HALO, :] = ctx_ref[0]

    @pl.when(t > 0)
    def _():
        up_ref[0:HALO, :] = ah_ref[...].astype(F32) * sbh_ref[...].astype(F32)

    n_shifted = tq + HALO - SUBLANES
    for s in range(1, SUBLANES):
        us_ref[s - 1, 0:n_shifted, :] = up_ref[s:s + n_shifted, :]

    groups = CONV_ROWS // SUBLANES
    for lc in range(D_CONV // CONV_LANES):
        lanes = slice(lc * CONV_LANES, (lc + 1) * CONV_LANES)

        def row_block(rb, carry, lanes=lanes):
            r0 = pl.multiple_of(rb * CONV_ROWS, CONV_ROWS)
            parts = [None, None]
            for s in range(SUBLANES):
                offs = _conv_offsets(s)
                lo = offs[0] - s
                rows = pl.ds(r0 + lo, offs[-1] - s + CONV_ROWS - lo)
                win = up_ref[rows, lanes] if s == 0 else us_ref[s - 1, rows, lanes]
                for o in offs:
                    tap = win[o - s - lo:o - s - lo + CONV_ROWS]
                    w = cw_ref[o - (HALO - CONV_LEFT), :, lanes]
                    term = tap.reshape(groups, SUBLANES, CONV_LANES) * w[None]
                    parts[s % 2] = term if parts[s % 2] is None else parts[s % 2] + term
            acc = parts[0] + parts[1]
            y_ref[pl.ds(r0, CONV_ROWS), lanes] = acc.reshape(CONV_ROWS, CONV_LANES) + cp_ref[0:1, lanes]
            return carry

        lax.fori_loop(0, tq // CONV_ROWS, row_block, 0)

    yn = _layer_norm(y_ref[...], cp_ref[1:2, :], cp_ref[2:3, :])
    sw = yn * _sigmoid(yn)
    pw = jnp.dot(sw.astype(BF16), wpw_ref[0], preferred_element_type=F32)
    o_ref[...] = (pw * g_ref[...].astype(F32)).astype(BF16)

    @pl.when(t == n_t - 1)
    def _():
        st_ref[0] = up_ref[HALO + tq - CONV_LEFT:HALO + tq, :]


def _conv_branch(h, ctx, cw, cp, wpw, layer, *, n_seq, t_len, tq):
    assert t_len % tq == 0 and tq % HALO == 0 and tq >= HALO
    n_t = t_len // tq
    m = n_seq * t_len

    def row_map(col):
        return lambda b, t: (b * n_t + t, col)

    def halo_map(col):
        return lambda b, t: (jnp.maximum((b * t_len + t * tq) // HALO - 1, 0), col)

    return pl.pallas_call(
        functools.partial(_conv_kernel, tq=tq, n_t=n_t),
        grid=(n_seq, n_t),
        in_specs=[
            pl.BlockSpec((tq, D_CONV), row_map(0)),
            pl.BlockSpec((tq, D_CONV), row_map(1)),
            pl.BlockSpec((tq, D_CONV), row_map(2)),
            pl.BlockSpec((HALO, D_CONV), halo_map(0)),
            pl.BlockSpec((HALO, D_CONV), halo_map(1)),
            pl.BlockSpec((1, HALO, D_CONV), lambda b, t: (b, 0, 0)),
            pl.BlockSpec((CONV_W, SUBLANES, D_CONV), lambda b, t: (0, 0, 0)),
            pl.BlockSpec((8, D_CONV), lambda b, t: (0, 0)),
            pl.BlockSpec((1, D_CONV, D_CONV), lambda b, t: (layer, 0, 0)),
        ],
        out_specs=[
            pl.BlockSpec((tq, D_CONV), lambda b, t: (b * n_t + t, 0)),
            pl.BlockSpec((1, CONV_LEFT, D_CONV), lambda b, t: (b, 0, 0)),
        ],
        out_shape=[
            jax.ShapeDtypeStruct((m, D_CONV), BF16),
            jax.ShapeDtypeStruct((n_seq, CONV_LEFT, D_CONV), F32),
        ],
        scratch_shapes=[
            pltpu.VMEM((HALO + tq, D_CONV), F32),
            pltpu.VMEM((SUBLANES - 1, HALO + tq, D_CONV), F32),
            pltpu.VMEM((tq, D_CONV), F32),
        ],
        compiler_params=_compiler_params(2),
        name="conv_branch",
    )(h, h, h, h, h, ctx, cw, cp, wpw)


def _qk(q, k):
    return lax.dot_general(q, k, (((1,), (1,)), ((), ())), preferred_element_type=F32)


def _attend(qs, ks, vs, biases):
    ss = [_qk(q, k) * SM_SCALE for q, k in zip(qs, ks)]
    ss = [s if b is None else s + b for s, b in zip(ss, biases)]
    ps, ls = [], []
    for s in ss:
        m = jnp.max(s, axis=-1, keepdims=True)
        p = jnp.exp(s - m)
        ls.append(jnp.sum(p, axis=-1, keepdims=True))
        ps.append(p.astype(BF16))
    return [jnp.dot(p, v, preferred_element_type=F32) / l for p, v, l in zip(ps, vs, ls)]


def _head_cols(h):
    return slice(h * HEAD_DIM, (h + 1) * HEAD_DIM)


def _memory_attention(mq_ref, mk_ref, mv_ref, mg_ref, o_ref):
    heads = range(N_HEADS)
    os = _attend([mq_ref[:, _head_cols(h)] for h in heads], [mk_ref[0, 0, :, _head_cols(h)] for h in heads],
                 [mv_ref[0, 0, :, _head_cols(h)] for h in heads], [None] * N_HEADS)
    for h in heads:
        o_ref[:, D_ATT + h * HEAD_DIM:D_ATT + (h + 1) * HEAD_DIM] = (
            os[h] * mg_ref[:, _head_cols(h)].astype(F32)).astype(BF16)


def _attn_prompt_kernel(q_ref, k_ref, v_ref, ag_ref, mq_ref, mg_ref, mk_ref, mv_ref, bias_ref,
                        o_ref, *, tq):
    t = pl.program_id(1)
    n_q = tq // Q_TILE
    heads = range(N_HEADS)

    def q_tile(qi, carry):
        c0 = (t * n_q + qi) * Q_CHUNKS
        start = pl.multiple_of(jnp.maximum(c0 * CHUNK - ATT_WIN, 0), Q_TILE)
        variant = jnp.minimum(c0 // Q_CHUNKS, N_BIAS_VARIANTS - 1)
        rows = pl.ds(pl.multiple_of(qi * Q_TILE, Q_TILE), Q_TILE)
        band = pl.ds(start, Q_BAND)
        os = _attend([q_ref[rows, _head_cols(h)] for h in heads], [k_ref[band, _head_cols(h)] for h in heads],
                     [v_ref[band, _head_cols(h)] for h in heads], [bias_ref[variant, h] for h in heads])
        for h in heads:
            o_ref[rows, _head_cols(h)] = (os[h] * ag_ref[rows, _head_cols(h)].astype(F32)).astype(BF16)
        return carry

    lax.fori_loop(0, n_q, q_tile, 0)
    _memory_attention(mq_ref, mk_ref, mv_ref, mg_ref, o_ref)


def _attn_prompt(h, mk, mv, layer, bias, *, n_seq, t_len, tq, col0):
    assert t_len % tq == 0 and tq % Q_TILE == 0 and t_len >= Q_BAND
    n_t = t_len // tq

    def row_map(col):
        return lambda b, t: (b * n_t + t, col0 + col)

    return pl.pallas_call(
        functools.partial(_attn_prompt_kernel, tq=tq),
        grid=(n_seq, n_t),
        in_specs=[
            pl.BlockSpec((tq, D_ATT), row_map(COL_Q)),
            pl.BlockSpec((t_len, D_ATT), lambda b, t: (b, col0 + COL_K)),
            pl.BlockSpec((t_len, D_ATT), lambda b, t: (b, col0 + COL_V)),
            pl.BlockSpec((tq, D_ATT), row_map(COL_AG)),
            pl.BlockSpec((tq, D_ATT), row_map(COL_MQ)),
            pl.BlockSpec((tq, D_ATT), row_map(COL_MG)),
            pl.BlockSpec((1, 1, N_MEM, D_ATT), lambda b, t: (layer, b, 0, 0)),
            pl.BlockSpec((1, 1, N_MEM, D_ATT), lambda b, t: (layer, b, 0, 0)),
            pl.BlockSpec(bias.shape, lambda b, t: (0, 0, 0, 0)),
        ],
        out_specs=pl.BlockSpec((tq, 2 * D_ATT), lambda b, t: (b * n_t + t, 0)),
        out_shape=jax.ShapeDtypeStruct((n_seq * t_len, 2 * D_ATT), BF16),
        compiler_params=_compiler_params(2),
        name="attn_prompt",
    )(h, h, h, h, h, h, mk, mv, bias)


def _attn_sample_kernel(q_ref, k_ref, v_ref, ag_ref, mq_ref, mg_ref, kc_ref, vc_ref, mk_ref, mv_ref,
                        bias_ref, o_ref):
    heads = range(N_HEADS)
    ks = [jnp.concatenate([kc_ref[0, 0, :, _head_cols(h)], k_ref[:, _head_cols(h)]], axis=0) for h in heads]
    vs = [jnp.concatenate([vc_ref[0, 0, :, _head_cols(h)], v_ref[:, _head_cols(h)]], axis=0) for h in heads]
    os = _attend([q_ref[:, _head_cols(h)] for h in heads], ks, vs, [bias_ref[h] for h in heads])
    for h in heads:
        o_ref[:, _head_cols(h)] = (os[h] * ag_ref[:, _head_cols(h)].astype(F32)).astype(BF16)
    _memory_attention(mq_ref, mk_ref, mv_ref, mg_ref, o_ref)


def _attn_sample(h, kc, vc, mk, mv, layer, bias, *, n_seq):
    def row_map(col):
        return lambda b: (b, col)

    seq_spec = lambda n: pl.BlockSpec((1, 1, n, D_ATT), lambda b: (layer, b, 0, 0))
    return pl.pallas_call(
        _attn_sample_kernel,
        grid=(n_seq,),
        in_specs=[
            pl.BlockSpec((CHUNK, D_ATT), row_map(COL_Q)),
            pl.BlockSpec((CHUNK, D_ATT), row_map(COL_K)),
            pl.BlockSpec((CHUNK, D_ATT), row_map(COL_V)),
            pl.BlockSpec((CHUNK, D_ATT), row_map(COL_AG)),
            pl.BlockSpec((CHUNK, D_ATT), row_map(COL_MQ)),
            pl.BlockSpec((CHUNK, D_ATT), row_map(COL_MG)),
            seq_spec(ATT_WIN), seq_spec(ATT_WIN), seq_spec(N_MEM), seq_spec(N_MEM),
            pl.BlockSpec(bias.shape, lambda b: (0, 0, 0)),
        ],
        out_specs=pl.BlockSpec((CHUNK, 2 * D_ATT), lambda b: (b, 0)),
        out_shape=jax.ShapeDtypeStruct((n_seq * CHUNK, 2 * D_ATT), BF16),
        compiler_params=_compiler_params(1),
        name="attn_sample",
    )(h, h, h, h, h, h, kc, vc, mk, mv, bias)


OUT_COLS = 512


def _out_proj_kernel(cm_ref, am_ref, x_ref, w_ref, gb_ref, y_ref, z_ref, *, alpha):
    tm = x_ref.shape[0]
    row_sum = jnp.zeros((tm, 1), F32)
    for n in range(D_MODEL // OUT_COLS):
        cols = slice(n * OUT_COLS, (n + 1) * OUT_COLS)
        z = jnp.dot(cm_ref[...], w_ref[0, 0:D_CONV, cols], preferred_element_type=F32)
        z = z + jnp.dot(am_ref[...], w_ref[0, D_CONV:D_MIX, cols], preferred_element_type=F32)
        z = alpha * x_ref[:, cols] + z
        z_ref[:, cols] = z
        row_sum = row_sum + jnp.sum(z, axis=-1, keepdims=True)
    mu = row_sum * (1.0 / D_MODEL)
    sq_sum = jnp.zeros((tm, 1), F32)
    for n in range(D_MODEL // OUT_COLS):
        cols = slice(n * OUT_COLS, (n + 1) * OUT_COLS)
        d = z_ref[:, cols] - mu
        sq_sum = sq_sum + jnp.sum(d * d, axis=-1, keepdims=True)
    inv = lax.rsqrt(sq_sum * (1.0 / D_MODEL) + LN_EPS)
    for n in range(D_MODEL // OUT_COLS):
        cols = slice(n * OUT_COLS, (n + 1) * OUT_COLS)
        y_ref[:, cols] = (z_ref[:, cols] - mu) * inv * gb_ref[0:1, cols] + gb_ref[1:2, cols]


def _out_proj(cm, am, x, w, layer, gb, *, alpha):
    m = x.shape[0]
    tm = min(512, m)
    assert m % tm == 0
    return pl.pallas_call(
        functools.partial(_out_proj_kernel, alpha=alpha),
        grid=(m // tm,),
        in_specs=[
            pl.BlockSpec((tm, D_CONV), lambda i: (i, 0)),
            pl.BlockSpec((tm, 2 * D_ATT), lambda i: (i, 0)),
            pl.BlockSpec((tm, D_MODEL), lambda i: (i, 0)),
            pl.BlockSpec((1, D_MIX, D_MODEL), lambda i: (layer, 0, 0)),
            pl.BlockSpec((8, D_MODEL), lambda i: (0, 0)),
        ],
        out_specs=pl.BlockSpec((tm, D_MODEL), lambda i: (i, 0)),
        out_shape=jax.ShapeDtypeStruct((m, D_MODEL), F32),
        scratch_shapes=[pltpu.VMEM((tm, D_MODEL), F32)],
        compiler_params=_compiler_params(1),
        name="out_proj",
    )(cm, am, x, w, gb)


OUT_ROWS = 256


def _out_proj_conv_kernel(yc_ref, g_ref, am_ref, x_ref, cp_ref, wpw_ref, w_ref, gb_ref, y_ref, z_ref, *, alpha):
    tm = x_ref.shape[0]
    n_rows = min(OUT_ROWS, tm)
    for r0 in range(0, tm, n_rows):
        rows = slice(r0, r0 + n_rows)
        yn = _layer_norm(yc_ref[rows, :], cp_ref[1:2, :], cp_ref[2:3, :])
        sw = yn * _sigmoid(yn)
        pw = jnp.dot(sw.astype(BF16), wpw_ref[0], preferred_element_type=F32)
        cm = (pw * g_ref[rows, :].astype(F32)).astype(BF16)
        am = am_ref[rows, :]
        row_sum = jnp.zeros((n_rows, 1), F32)
        for n in range(D_MODEL // OUT_COLS):
            cols = slice(n * OUT_COLS, (n + 1) * OUT_COLS)
            z = jnp.dot(cm, w_ref[0, 0:D_CONV, cols], preferred_element_type=F32)
            z = z + jnp.dot(am, w_ref[0, D_CONV:D_MIX, cols], preferred_element_type=F32)
            z = z + alpha * x_ref[rows, cols]
            z_ref[rows, cols] = z
            row_sum = row_sum + jnp.sum(z, axis=-1, keepdims=True)
        mu = row_sum * (1.0 / D_MODEL)
        sq_sum = jnp.zeros((n_rows, 1), F32)
        for n in range(D_MODEL // OUT_COLS):
            cols = slice(n * OUT_COLS, (n + 1) * OUT_COLS)
            d = z_ref[rows, cols] - mu
            sq_sum = sq_sum + jnp.sum(d * d, axis=-1, keepdims=True)
        inv = lax.rsqrt(sq_sum * (1.0 / D_MODEL) + LN_EPS)
        for n in range(D_MODEL // OUT_COLS):
            cols = slice(n * OUT_COLS, (n + 1) * OUT_COLS)
            y_ref[rows, cols] = (z_ref[rows, cols] - mu) * inv * gb_ref[0:1, cols] + gb_ref[1:2, cols]


def _out_proj_conv(yc, h, am, x, cp, wpw, w, layer, gb, *, alpha):
    m = x.shape[0]
    tm = min(512, m)
    assert m % tm == 0 and tm % min(OUT_ROWS, tm) == 0
    rows_of = lambda width: pl.BlockSpec((tm, width), lambda i: (i, 0))
    whole = lambda shape: pl.BlockSpec(shape, lambda i: (0, 0), pipeline_mode=ONCE)
    of_layer = lambda shape: pl.BlockSpec((1,) + shape, lambda i: (layer, 0, 0), pipeline_mode=ONCE)
    return pl.pallas_call(
        functools.partial(_out_proj_conv_kernel, alpha=alpha),
        grid=(m // tm,),
        in_specs=[
            rows_of(D_CONV), rows_of(D_CONV), rows_of(2 * D_ATT), rows_of(D_MODEL),
            whole((8, D_CONV)), of_layer((D_CONV, D_CONV)), of_layer((D_MIX, D_MODEL)), whole((8, D_MODEL)),
        ],
        out_specs=rows_of(D_MODEL),
        out_shape=jax.ShapeDtypeStruct((m, D_MODEL), F32),
        scratch_shapes=[pltpu.VMEM((tm, D_MODEL), F32)],
        compiler_params=_compiler_params(1),
        name="out_proj_conv",
    )(yc, h, am, x, cp, wpw, w, gb)


def _band_bias(table):
    n_h = table.shape[0]
    n_u = Q_TILE + Q_BAND - 1
    n_flat = ATT_WIN + Q_TILE - REL_CLIP
    lo = REL_CLIP - (Q_TILE - 1)
    assert lo >= 0 and n_flat + 2 * REL_CLIP - lo == n_u
    u = jnp.concatenate([jnp.broadcast_to(table[:, 2 * REL_CLIP:], (n_h, n_flat)),
                         jnp.flip(table[:, lo:2 * REL_CLIP], axis=1)], axis=1).astype(F32)
    u = jnp.pad(u, ((0, 0), (0, 1)))
    skew = jnp.broadcast_to(u[:, None, :], (n_h, Q_TILE, n_u + 1)).reshape(n_h, Q_TILE * (n_u + 1))
    full = skew[:, Q_TILE - 1:Q_TILE - 1 + Q_TILE * n_u].reshape(n_h, Q_TILE, n_u)[:, :, :Q_BAND]
    qc = jnp.arange(Q_TILE)[:, None] // CHUNK
    kc = jnp.arange(Q_BAND)[None, :] // CHUNK
    full = jnp.where((kc >= qc) & (kc <= qc + N_LEFT_CHUNKS), full, MASK_VALUE)
    variants = []
    for v in range(N_BIAS_VARIANTS - 1):
        off = ATT_WIN - v * Q_TILE
        pad = jnp.full((n_h, Q_TILE, off), MASK_VALUE, F32)
        variants.append(jnp.concatenate([full[:, :, off:], pad], axis=-1))
    variants.append(full)
    return jnp.stack(variants)


def _pack_rows(rows, width):
    return jnp.concatenate([jnp.stack(rows).astype(F32), jnp.zeros((8 - len(rows), width), F32)])


def kernel(x_prompt, x_sample, mem_prompt, cache_conv, cache_att_k, cache_att_v, cache_mem_k, cache_mem_v,
           w_in, conv_w, conv_b, conv_ln_g, conv_ln_b, w_pw, rel_table, w_mem_kv, w_out, ln_g, ln_b):
    n_p, t_p, _ = x_prompt.shape
    n_s, t_s, _ = x_sample.shape
    depth = w_in.shape[0]
    n_mem = mem_prompt.shape[1]
    assert t_s == CHUNK and n_mem == N_MEM and cache_att_k.shape[2] == ATT_WIN
    alpha = (2 * depth) ** 0.25
    keep = min(ATT_WIN, t_p)

    w_in_b = w_in.astype(BF16)
    w_pw_b = w_pw.astype(BF16)
    w_out_b = w_out.astype(BF16)
    w_mkv_b = w_mem_kv.astype(BF16)

    mk_p, mv_p, mk_pb, mv_pb = _mem_kv(mem_prompt.reshape(n_p * n_mem, D_MODEL).astype(BF16), w_mkv_b)
    mk_pb = mk_pb.reshape(depth, n_p, n_mem, D_ATT)
    mv_pb = mv_pb.reshape(depth, n_p, n_mem, D_ATT)

    xp = x_prompt.reshape(n_p * t_p, D_MODEL)
    xs = x_sample.reshape(n_s * t_s, D_MODEL)
    zero_ctx = jnp.zeros((n_p, HALO, D_CONV), F32)
    cache_ctx = jnp.pad(cache_conv, ((0, 0), (0, 0), (HALO - CONV_LEFT, 0), (0, 0)))
    kc_b = cache_att_k.reshape(depth, n_s, ATT_WIN, D_ATT).astype(BF16)
    vc_b = cache_att_v.reshape(depth, n_s, ATT_WIN, D_ATT).astype(BF16)
    mk_sb = cache_mem_k.reshape(depth, n_s, n_mem, D_ATT).astype(BF16)
    mv_sb = cache_mem_v.reshape(depth, n_s, n_mem, D_ATT).astype(BF16)

    conv_p, kp_l, vp_l, conv_s, ks_l, vs_l = [], [], [], [], [], []
    for l in range(depth):
        cw = jnp.broadcast_to(conv_w[l][:, None, :], (CONV_W, SUBLANES, D_CONV))
        cp = _pack_rows([conv_b[l], conv_ln_g[l], conv_ln_b[l]], D_CONV)
        gb = _pack_rows([ln_g[l], ln_b[l]], D_MODEL)
        bias = _band_bias(rel_table[l])

        h, yc, cs, k_new, v_new = _in_proj_conv(xp, w_in_b, l, zero_ctx, cw, cp, seq_rows=t_p, keep=keep)
        am = _attn_prompt(h, mk_pb, mv_pb, l, bias, n_seq=n_p, t_len=t_p, tq=512, col0=-N_GLU_BLOCKS)
        xp = _out_proj_conv(yc, h, am, xp, cp, w_pw_b, w_out_b, l, gb, alpha=alpha)
        conv_p.append(cs)
        kp_l.append(k_new)
        vp_l.append(v_new)

        h, k_new, v_new = _in_proj(xs, w_in_b, l, seq_rows=n_s * t_s, keep=n_s * t_s)
        cm, cs = _conv_branch(h, cache_ctx[l], cw, cp, w_pw_b, l, n_seq=n_s, t_len=t_s, tq=t_s)
        am = _attn_sample(h, kc_b, vc_b, mk_sb, mv_sb, l, bias[-1, :, :CHUNK, :BAND], n_seq=n_s)
        xs = _out_proj(cm, am, xs, w_out_b, l, gb, alpha=alpha)
        conv_s.append(cs)
        ks_l.append(k_new)
        vs_l.append(v_new)

    mem_shape = (depth, n_p, n_mem, N_HEADS, HEAD_DIM)
    kv_p_shape = (depth, n_p, keep, N_HEADS, HEAD_DIM)
    kv_s_shape = (depth, n_s, t_s, N_HEADS, HEAD_DIM)
    return (xp.reshape(n_p, t_p, D_MODEL), xs.reshape(n_s, t_s, D_MODEL), jnp.stack(conv_p),
            jnp.stack(kp_l).reshape(kv_p_shape), jnp.stack(vp_l).reshape(kv_p_shape),
            mk_p.reshape(mem_shape), mv_p.reshape(mem_shape),
            jnp.stack(conv_s), jnp.stack(ks_l).reshape(kv_s_shape), jnp.stack(vs_l).reshape(kv_s_shape))
```
